```python
import jax, jax.numpy as jnp
from jax import lax
import numpy as np

D_MODEL = 1024
BATCH = 16
SEQ = 4096
DEPTH = 4
DEC_BATCH = 16
DEC_SEQ = 16
PAST_LEN = 1024

CHUNK = 64
HGRN_BLOCK = CHUNK // 4
A_HEADS = 4
A_DK = 128
A_DV = 128
A_WIDTH = A_HEADS * A_DV
B_WIDTH = D_MODEL // 2
CONV_W = 3
D_FF = 2816
N_IN = 4 * A_WIDTH + 3 * B_WIDTH
EPS = 1e-6

kernel_name = 'hybrid_hgrn2_shortconv_convffn_stream_step'


def rmsnorm(x, g):
    xf = x.astype(jnp.float32)
    y = xf * lax.rsqrt(jnp.mean(xf * xf, axis=-1, keepdims=True) + EPS)
    return (y * g.astype(jnp.float32)).astype(x.dtype)


def causal_dwconv(u, buf, w):
    L = u.shape[1]
    full = jnp.concatenate([buf.astype(u.dtype), u], axis=1)
    y = full[:, 0:L] * w[0]
    for k in range(1, CONV_W):
        y = y + full[:, k:k + L] * w[k]
    return y, full[:, -(CONV_W - 1):]


def hgrn_lower_bounds(lb_logits):
    p = jax.nn.softmax(lb_logits.astype(jnp.float32), axis=0)
    g = jnp.cumsum(p, axis=0)
    return g - g[0:1]


def hgrn_block(S, q, k, v, logf):
    L = q.shape[1]
    b = jnp.cumsum(logf, axis=1)
    o_inter = jnp.einsum('blhk,bhkv->blhv', q * jnp.exp(b), S)
    causal = (jnp.arange(L)[:, None] >= jnp.arange(L)[None, :])[None, :, :, None, None]
    decay = jnp.exp(jnp.where(causal, b[:, :, None] - b[:, None, :], -jnp.inf))
    A = jnp.einsum('btshk,bthk->btsh', decay * k[:, None], q)
    o_intra = jnp.einsum('btsh,bshv->bthv', A, v)
    b_last = b[:, -1]
    kd = k * jnp.exp(b_last[:, None] - b)
    S_new = jnp.exp(b_last)[..., None] * S + jnp.einsum('blhk,blhv->bhkv', kd, v)
    return S_new, o_inter + o_intra


def hgrn_scan(S0, q, k, v, logf):
    B, T = q.shape[:2]
    nb = T // HGRN_BLOCK

    def to_blocks(a):
        return a.reshape(B, nb, HGRN_BLOCK, *a.shape[2:]).swapaxes(0, 1)

    def step(S, blk):
        return hgrn_block(S, *blk)

    S, o = lax.scan(step, S0, (to_blocks(q), to_blocks(k), to_blocks(v), to_blocks(logf)))
    return S, o.swapaxes(0, 1).reshape(B, T, *o.shape[3:])


def token_mixer(h, S0, conv_buf, lb, w_in, a_norm, w_pa, conv_w, w_pb, w_bgate, w_o):
    B, L, _ = h.shape
    a, bw = A_WIDTH, B_WIDTH
    q, f, i, g, bg, cg, xv = jnp.split(h @ w_in, [a, 2 * a, 3 * a, 4 * a, 4 * a + bw, 4 * a + 2 * bw], axis=-1)
    shp = (B, L, A_HEADS, A_DK)
    qa = jax.nn.silu(q.astype(jnp.float32)).reshape(shp)
    z = f.astype(jnp.float32).reshape(shp)
    lbh = lb.reshape(A_HEADS, A_DK)
    logf = jnp.logaddexp(jnp.log(lbh), jnp.log1p(-lbh) + jax.nn.log_sigmoid(z))
    ka = (1.0 - lbh) * jax.nn.sigmoid(-z)
    va = i.astype(jnp.float32).reshape(B, L, A_HEADS, A_DV)
    S0 = S0.astype(jnp.float32)
    if L > CHUNK:
        S, o = hgrn_scan(S0, qa, ka, va, logf)
    else:
        S, o = hgrn_block(S0, qa, ka, va, logf)
    o = rmsnorm(o, a_norm) * jax.nn.silu(g.astype(jnp.float32).reshape(B, L, A_HEADS, A_DV))
    y_a = o.reshape(B, L, A_WIDTH).astype(h.dtype) @ w_pa
    u, new_buf = causal_dwconv(cg * xv, conv_buf, conv_w)
    y_b = (bg * u) @ w_pb
    ga, gb = jnp.split(jax.nn.sigmoid(h @ w_bgate), 2, axis=-1)
    return (ga * y_a + gb * y_b) @ w_o, S, new_buf


def conv_ffn(h, buf, w_up, conv_w, w_down):
    gate, val = jnp.split(h @ w_up, 2, axis=-1)
    gate, new_buf = causal_dwconv(gate, buf, conv_w)
    return (jax.nn.silu(gate) * val) @ w_down, new_buf


def trunk(x, c, S_in, cb_in, fb_in, lbs, params):
    (w_ada, b_ada, g_pre_mix, g_post_mix, g_pre_ffn, g_post_ffn, w_in, a_norm, w_pa,
     conv_w, w_pb, w_bgate, w_o, w_up, ffn_conv_w, w_down) = params
    cs = jax.nn.silu(c)
    new_S, new_cb, new_fb = [], [], []
    for l in range(DEPTH):
        ada = cs @ w_ada[l] + b_ada[l]
        sh1, sc1, gt1, sh2, sc2, gt2 = jnp.split(ada[:, None, :], 6, axis=-1)
        h = rmsnorm(x, g_pre_mix[l]) * (1.0 + sc1) + sh1
        m, S, cb = token_mixer(h, S_in[l], cb_in[l], lbs[l], w_in[l], a_norm[l], w_pa[l],
                               conv_w[l], w_pb[l], w_bgate[l], w_o[l])
        x = x + gt1 * rmsnorm(m, g_post_mix[l])
        h = rmsnorm(x, g_pre_ffn[l]) * (1.0 + sc2) + sh2
        ff, fb = conv_ffn(h, fb_in[l], w_up[l], ffn_conv_w[l], w_down[l])
        x = x + gt2 * rmsnorm(ff, g_post_ffn[l])
        new_S.append(S)
        new_cb.append(cb)
        new_fb.append(fb)
    return x, jnp.stack(new_S).astype(x.dtype), jnp.stack(new_cb), jnp.stack(new_fb)


def setup_inputs(seed: int = 0) -> dict:
    key = jax.random.key(seed)
    ks = jax.random.split(key, 32)
    D = D_MODEL

    def nrm(k, shape, scale):
        return jax.random.normal(k, shape, jnp.float32) * scale

    return {
        'x_prompt': nrm(ks[0], (BATCH, SEQ, D), 1.0),
        'x_sample': nrm(ks[1], (DEC_BATCH, DEC_SEQ, D), 1.0),
        'state_hgrn': nrm(ks[2], (DEPTH, DEC_BATCH, A_HEADS, A_DK, A_DV), 0.5),
        'state_conv': nrm(ks[3], (DEPTH, DEC_BATCH, CONV_W - 1, B_WIDTH), 1.0),
        'state_ffn_conv': nrm(ks[4], (DEPTH, DEC_BATCH, CONV_W - 1, D_FF), 1.0),
        'c_prompt': nrm(ks[5], (BATCH, D), 1.0),
        'c_sample': nrm(ks[6], (DEC_BATCH, D), 1.0),
        'w_ada': nrm(ks[7], (DEPTH, D, 6 * D), 0.5 * D ** -0.5),
        'b_ada': nrm(ks[8], (DEPTH, 6 * D), 0.02),
        'g_pre_mix': 1.0 + nrm(ks[9], (DEPTH, D), 0.05),
        'g_post_mix': 1.0 + nrm(ks[10], (DEPTH, D), 0.05),
        'g_pre_ffn': 1.0 + nrm(ks[11], (DEPTH, D), 0.05),
        'g_post_ffn': 1.0 + nrm(ks[12], (DEPTH, D), 0.05),
        'w_in': nrm(ks[13], (DEPTH, D, N_IN), D ** -0.5),
        'hgrn_lb_logits': nrm(ks[14], (DEPTH, A_WIDTH), 1.0),
        'hgrn_norm': 1.0 + nrm(ks[15], (DEPTH, A_DV), 0.05),
        'w_pa': nrm(ks[16], (DEPTH, A_WIDTH, D), A_WIDTH ** -0.5),
        'conv_w': nrm(ks[17], (DEPTH, CONV_W, B_WIDTH), CONV_W ** -0.5),
        'w_pb': nrm(ks[18], (DEPTH, B_WIDTH, D), B_WIDTH ** -0.5),
        'w_bgate': nrm(ks[19], (DEPTH, D, 2 * D), D ** -0.5),
        'w_o': nrm(ks[20], (DEPTH, D, D), D ** -0.5),
        'w_up': nrm(ks[21], (DEPTH, D, 2 * D_FF), D ** -0.5),
        'ffn_conv_w': nrm(ks[22], (DEPTH, CONV_W, D_FF), CONV_W ** -0.5),
        'w_down': nrm(ks[23], (DEPTH, D_FF, D), D_FF ** -0.5),
    }


def reference(x_prompt, x_sample, state_hgrn, state_conv, state_ffn_conv, c_prompt, c_sample,
              w_ada, b_ada, g_pre_mix, g_post_mix, g_pre_ffn, g_post_ffn, w_in, hgrn_lb_logits,
              hgrn_norm, w_pa, conv_w, w_pb, w_bgate, w_o, w_up, ffn_conv_w, w_down):
    params = (w_ada, b_ada, g_pre_mix, g_post_mix, g_pre_ffn, g_post_ffn, w_in, hgrn_norm, w_pa,
              conv_w, w_pb, w_bgate, w_o, w_up, ffn_conv_w, w_down)
    lbs = hgrn_lower_bounds(hgrn_lb_logits)
    B = x_prompt.shape[0]
    dt = x_prompt.dtype
    S0 = jnp.zeros((DEPTH, B, A_HEADS, A_DK, A_DV), jnp.float32)
    cb0 = jnp.zeros((DEPTH, B, CONV_W - 1, B_WIDTH), dt)
    fb0 = jnp.zeros((DEPTH, B, CONV_W - 1, D_FF), dt)
    y_prompt, s_hgrn_p, s_conv_p, s_ffn_p = trunk(x_prompt, c_prompt, S0, cb0, fb0, lbs, params)
    y_sample, s_hgrn_s, s_conv_s, s_ffn_s = trunk(x_sample, c_sample, state_hgrn, state_conv,
                                                  state_ffn_conv, lbs, params)
    return (y_prompt, y_sample, s_hgrn_p, s_conv_p, s_ffn_p, s_hgrn_s, s_conv_s, s_ffn_s)
```

```python
import functools

import jax
import jax.numpy as jnp
from jax import lax
from jax.experimental import pallas as pl
from jax.experimental.pallas import tpu as pltpu

HEADS = 4
DK = 128
DV = 128
AW = HEADS * DV
CONV_HIST = 2
EPS = 1e-6
SUBLANES = 8
HIST_PAD = 8
BF = jnp.bfloat16
F32 = jnp.float32
VMEM_LIMIT = 56 * 1024 * 1024


def _sigmoid(x):
    return 1.0 / (1.0 + jnp.exp(-x))


def _rms(x, g):
    ms = jnp.mean(x * x, axis=-1, keepdims=True)
    return x * lax.rsqrt(ms + EPS) * g


def _dot(a, b):
    return jnp.dot(a, b, preferred_element_type=F32)


def _dot_nt(a, b):
    return lax.dot_general(a, b, (((1,), (1,)), ((), ())), preferred_element_type=F32)


def _dot_tn(a, b):
    return lax.dot_general(a, b, (((0,), (0,)), ((), ())), preferred_element_type=F32)


def _levels(chunk):
    lv, g = [], SUBLANES
    while 2 * g <= chunk:
        lv.append(g)
        g *= 2
    return lv


def _hgrn_chunk(qa, ka, va, lf, st):
    c = qa.shape[0]
    row = lax.broadcasted_iota(jnp.int32, (c, AW), 0)
    b = lf
    d = 1
    while d < c:
        b = b + jnp.where(row >= d, pltpu.roll(b, d, 0), 0.0)
        d *= 2
    blast = b[c - 1:c, :]
    qc = (qa * jnp.exp(b)).astype(BF)
    kc = (ka * jnp.exp(blast - b)).astype(BF)
    vb = va.astype(BF)

    levels = _levels(c)
    xs = []
    for g in levels:
        ref_rows = [jnp.broadcast_to(b[p * 2 * g + g - 1:p * 2 * g + g, :], (2 * g, AW))
                    for p in range(c // (2 * g))]
        rg = ref_rows[0] if len(ref_rows) == 1 else jnp.concatenate(ref_rows, axis=0)
        eg = jnp.exp(-jnp.abs(b - rg))
        xs.append((jnp.where((row & g) != 0, qa, ka) * eg).astype(BF))

    ti = lax.broadcasted_iota(jnp.int32, (c, c), 0)
    si = lax.broadcasted_iota(jnp.int32, (c, c), 1)
    txs = ti ^ si
    masks = [((ti & g) != 0) & (txs >= g) & (txs < 2 * g) for g in levels]

    nblk = c // SUBLANES
    b3 = b.reshape(nblk, SUBLANES, AW)
    q3 = qa.reshape(nblk, SUBLANES, AW)
    k3 = ka.reshape(nblk, SUBLANES, AW)
    v3 = va.reshape(nblk, SUBLANES, AW)
    tin = lax.broadcasted_iota(jnp.int32, (nblk, SUBLANES, 1), 1)
    od = [jnp.zeros((nblk, SUBLANES, DV), F32) for _ in range(HEADS)]
    for s in range(SUBLANES):
        dec = jnp.exp(jnp.minimum(b3 - b3[:, s:s + 1, :], 0.0))
        p = q3 * (k3[:, s:s + 1, :] * dec)
        vs = v3[:, s:s + 1, :]
        for h in range(HEADS):
            sl = slice(h * DK, (h + 1) * DK)
            a = jnp.sum(p[:, :, sl], axis=-1, keepdims=True)
            a = jnp.where(tin >= s, a, 0.0)
            od[h] = od[h] + a * vs[:, :, sl]

    outs, new_st = [], []
    for h in range(HEADS):
        sl = slice(h * DK, (h + 1) * DK)
        o = _dot_nt(qc[:, sl], st[h].astype(BF))
        if levels:
            amat = jnp.zeros((c, c), F32)
            for xg, m in zip(xs, masks):
                amat = jnp.where(m, _dot_nt(xg[:, sl], xg[:, sl]), amat)
            o = o + _dot(amat.astype(BF), vb[:, sl])
        o = o + od[h].reshape(c, DV)
        outs.append(o)
        new_st.append(st[h] * jnp.exp(blast[:, sl]) + _dot_tn(vb[:, sl], kc[:, sl]))
    return jnp.concatenate(outs, axis=-1), new_st


def _causal_conv(ubuf_ref, hist_in_ref, hist_out_ref, u3, w, first):
    lt = u3.shape[1]
    lo = HIST_PAD - CONV_HIST

    @pl.when(first)
    def _():
        ubuf_ref[:, lo:HIST_PAD, :] = hist_in_ref[...]

    ubuf_ref[:, HIST_PAD:, :] = u3
    y = (ubuf_ref[:, lo:lo + lt, :] * w[0:1, :] + ubuf_ref[:, lo + 1:lo + 1 + lt, :] * w[1:2, :]
         + u3 * w[2:3, :])
    hist = ubuf_ref[:, lt + lo:lt + HIST_PAD, :]
    hist_out_ref[...] = hist
    ubuf_ref[:, lo:HIST_PAD, :] = hist
    return y


def _mixer_kernel(layer, nb, lt, chunk,
                  x_ref, ada_ref, sin_ref, cbin_ref, lbl_ref, gpre_ref, gpost_ref, anorm_ref,
                  win_ref, wpa_ref, convw_ref, wpb_ref, wbg_ref, wo_ref,
                  xo_ref, sout_ref, cbout_ref, st_ref, ubuf_ref):
    t = pl.program_id(1)
    first = t == 0
    last = t == pl.num_programs(1) - 1
    d = x_ref.shape[-1]
    r = nb * lt

    @pl.when(first)
    def _():
        for bi in range(nb):
            for h in range(HEADS):
                st_ref[bi, h] = sin_ref[bi, h].T

    x3 = x_ref[...]
    ada = ada_ref[...]
    sh1, sc1, gt1 = ada[0], ada[1], ada[2]
    h3 = _rms(x3, gpre_ref[...]) * (1.0 + sc1) + sh1
    hb = h3.reshape(r, d).astype(BF)
    proj = _dot(hb, win_ref[...])
    q, z, vi, gg = proj[:, 0:AW], proj[:, AW:2 * AW], proj[:, 2 * AW:3 * AW], proj[:, 3 * AW:4 * AW]
    bgt, cgt, xv = proj[:, 4 * AW:5 * AW], proj[:, 5 * AW:6 * AW], proj[:, 6 * AW:7 * AW]

    qa = q * _sigmoid(q)
    ez = jnp.exp(-jnp.abs(z))
    logsig = jnp.minimum(z, 0.0) - jnp.log1p(ez)
    rz = 1.0 / (1.0 + ez)
    signeg = jnp.where(z >= 0.0, ez * rz, rz)
    if layer == 0:
        lf = logsig
        ka = signeg
    else:
        lg = lbl_ref[...]
        e = jnp.exp(lg - jnp.max(lg, axis=0, keepdims=True))
        pr = e / jnp.sum(e, axis=0, keepdims=True)
        lb = jnp.sum(pr[1:layer + 1, :], axis=0, keepdims=True)
        a0 = jnp.log(lb)
        y = jnp.log1p(-lb) + logsig
        lf = jnp.maximum(a0, y) + jnp.log1p(jnp.exp(-jnp.abs(a0 - y)))
        ka = (1.0 - lb) * signeg

    o_parts = []
    for bi in range(nb):
        st = [st_ref[bi, h] for h in range(HEADS)]
        for ci in range(lt // chunk):
            r0 = bi * lt + ci * chunk
            rs = slice(r0, r0 + chunk)
            o, st = _hgrn_chunk(qa[rs], ka[rs], vi[rs], lf[rs], st)
            o_parts.append(o)
        for h in range(HEADS):
            st_ref[bi, h] = st[h]
    o = o_parts[0] if len(o_parts) == 1 else jnp.concatenate(o_parts, axis=0)

    @pl.when(last)
    def _():
        for bi in range(nb):
            for h in range(HEADS):
                sout_ref[bi, h] = st_ref[bi, h].T

    gs = gg * _sigmoid(gg)
    an = anorm_ref[...]
    on = jnp.concatenate(
        [_rms(o[:, h * DV:(h + 1) * DV], an) for h in range(HEADS)], axis=-1) * gs
    y_a = _dot(on.astype(BF), wpa_ref[...])

    cw = cbin_ref.shape[-1]
    u = _causal_conv(ubuf_ref, cbin_ref, cbout_ref, (cgt * xv).reshape(nb, lt, cw), convw_ref[...], first)
    y_b = _dot((bgt * u.reshape(r, cw)).astype(BF), wpb_ref[...])

    gates = _sigmoid(_dot(hb, wbg_ref[...]))
    mix = gates[:, :d] * y_a + gates[:, d:] * y_b
    m = _dot(mix.astype(BF), wo_ref[...])
    xo_ref[...] = x3 + gt1 * _rms(m, gpost_ref[...]).reshape(nb, lt, d)


def _ffn_kernel(nb, lt,
                x_ref, ada_ref, fbin_ref, gpre_ref, gpost_ref, wup_ref, convw_ref, wdn_ref,
                xo_ref, fbout_ref, ubuf_ref):
    first = pl.program_id(1) == 0
    d = x_ref.shape[-1]
    dff = fbin_ref.shape[-1]
    r = nb * lt
    x3 = x_ref[...]
    ada = ada_ref[...]
    sh2, sc2, gt2 = ada[3], ada[4], ada[5]
    h3 = _rms(x3, gpre_ref[...]) * (1.0 + sc2) + sh2
    up = _dot(h3.reshape(r, d).astype(BF), wup_ref[...])
    gate = _causal_conv(ubuf_ref, fbin_ref, fbout_ref, up[:, :dff].reshape(nb, lt, dff), convw_ref[...],
                        first).reshape(r, dff)
    act = gate * _sigmoid(gate) * up[:, dff:]
    ff = _dot(act.astype(BF), wdn_ref[...])
    xo_ref[...] = x3 + gt2 * _rms(ff, gpost_ref[...]).reshape(nb, lt, d)


def _ada_kernel(c_ref, w_ref, b_ref, o_ref):
    c = c_ref[...]
    cs = (c * _sigmoid(c)).astype(BF)
    o_ref[...] = _dot(cs, w_ref[...].astype(BF)) + b_ref[...]


def _ada_call(c_all, w_ada, b_ada):
    depth, d, d6 = w_ada.shape
    n = d6 // d
    bt = c_all.shape[0]
    return pl.pallas_call(
        _ada_kernel,
        grid=(depth, n),
        in_specs=[
            pl.BlockSpec((bt, d), lambda l, j: (0, 0)),
            pl.BlockSpec((None, d, d), lambda l, j: (l, 0, j)),
            pl.BlockSpec((None, None, 1, d), lambda l, j: (l, j, 0, 0)),
        ],
        out_specs=pl.BlockSpec((None, None, bt, d), lambda l, j: (l, j, 0, 0)),
        out_shape=jax.ShapeDtypeStruct((depth, n, bt, d), F32),
        compiler_params=pltpu.CompilerParams(dimension_semantics=("arbitrary", "arbitrary")),
        name="ada",
    )(c_all, w_ada, b_ada.reshape(depth, n, 1, d))


def _resident(shape, layer):
    nd = len(shape)
    return pl.BlockSpec((None,) + tuple(shape[1:]), lambda b, t: (layer,) + (0,) * (nd - 1),
                        pipeline_mode=pl.Buffered(1))


def _tile_plan(batch, length):
    if length >= 256:
        lt = 256
        assert length % lt == 0
        return 1, lt, 128
    assert length % SUBLANES == 0
    return batch, length, length


def _mixer_call(layer, x, ada, boff, s_in, cb_in, p):
    batch, length, d = x.shape
    nb, lt, chunk = _tile_plan(batch, length)
    assert boff % nb == 0 and batch % nb == 0
    bo = boff // nb
    cw = cb_in.shape[-1]
    row = lambda b, t: (b, t, 0)
    per_b4 = lambda b, t: (b, 0, 0, 0)
    per_b3 = lambda b, t: (b, 0, 0)
    in_specs = [
        pl.BlockSpec((nb, lt, d), row),
        pl.BlockSpec((None, 6, nb, 1, d), lambda b, t: (layer, 0, b + bo, 0, 0)),
        pl.BlockSpec((nb, HEADS, DK, DV), per_b4),
        pl.BlockSpec((nb, CONV_HIST, cw), per_b3),
        pl.BlockSpec(p["lb_logits"].shape, lambda b, t: (0, 0)),
        _resident(p["g_pre_mix"].shape, layer),
        _resident(p["g_post_mix"].shape, layer),
        _resident(p["hgrn_norm"].shape, layer),
        _resident(p["w_in"].shape, layer),
        _resident(p["w_pa"].shape, layer),
        _resident(p["conv_w"].shape, layer),
        _resident(p["w_pb"].shape, layer),
        _resident(p["w_bgate"].shape, layer),
        _resident(p["w_o"].shape, layer),
    ]
    out_specs = [
        pl.BlockSpec((nb, lt, d), row),
        pl.BlockSpec((nb, HEADS, DK, DV), per_b4),
        pl.BlockSpec((nb, CONV_HIST, cw), per_b3),
    ]
    out_shape = [
        jax.ShapeDtypeStruct(x.shape, F32),
        jax.ShapeDtypeStruct(s_in.shape, F32),
        jax.ShapeDtypeStruct(cb_in.shape, F32),
    ]
    return pl.pallas_call(
        functools.partial(_mixer_kernel, layer, nb, lt, chunk),
        grid=(batch // nb, length // lt),
        in_specs=in_specs,
        out_specs=out_specs,
        out_shape=out_shape,
        scratch_shapes=[
            pltpu.VMEM((nb, HEADS, DV, DK), F32),
            pltpu.VMEM((nb, lt + HIST_PAD, cw), F32),
        ],
        compiler_params=pltpu.CompilerParams(
            dimension_semantics=("arbitrary", "arbitrary"), vmem_limit_bytes=VMEM_LIMIT),
        name=f"mixer_l{layer}_n{nb}",
    )(x, ada, s_in, cb_in, p["lb_logits"], p["g_pre_mix"], p["g_post_mix"], p["hgrn_norm"],
      p["w_in"], p["w_pa"], p["conv_w"], p["w_pb"], p["w_bgate"], p["w_o"])


def _ffn_call(layer, x, ada, boff, fb_in, p):
    batch, length, d = x.shape
    nb, lt, _ = _tile_plan(batch, length)
    bo = boff // nb
    dff = fb_in.shape[-1]
    row = lambda b, t: (b, t, 0)
    per_b3 = lambda b, t: (b, 0, 0)
    in_specs = [
        pl.BlockSpec((nb, lt, d), row),
        pl.BlockSpec((None, 6, nb, 1, d), lambda b, t: (layer, 0, b + bo, 0, 0)),
        pl.BlockSpec((nb, CONV_HIST, dff), per_b3),
        _resident(p["g_pre_ffn"].shape, layer),
        _resident(p["g_post_ffn"].shape, layer),
        _resident(p["w_up"].shape, layer),
        _resident(p["ffn_conv_w"].shape, layer),
        _resident(p["w_down"].shape, layer),
    ]
    out_specs = [
        pl.BlockSpec((nb, lt, d), row),
        pl.BlockSpec((nb, CONV_HIST, dff), per_b3),
    ]
    out_shape = [
        jax.ShapeDtypeStruct(x.shape, F32),
        jax.ShapeDtypeStruct(fb_in.shape, F32),
    ]
    return pl.pallas_call(
        functools.partial(_ffn_kernel, nb, lt),
        grid=(batch // nb, length // lt),
        in_specs=in_specs,
        out_specs=out_specs,
        out_shape=out_shape,
        scratch_shapes=[pltpu.VMEM((nb, lt + HIST_PAD, dff), F32)],
        compiler_params=pltpu.CompilerParams(
            dimension_semantics=("arbitrary", "arbitrary"), vmem_limit_bytes=VMEM_LIMIT),
        name=f"ffn_l{layer}_n{nb}",
    )(x, ada, fb_in, p["g_pre_ffn"], p["g_post_ffn"], p["w_up"], p["ffn_conv_w"], p["w_down"])


def _trunk(x, ada, boff, s_in, cb_in, fb_in, p):
    depth = p["w_in"].shape[0]
    new_s, new_cb, new_fb = [], [], []
    for l in range(depth):
        x, s, cb = _mixer_call(l, x, ada, boff, s_in[l], cb_in[l], p)
        x, fb = _ffn_call(l, x, ada, boff, fb_in[l], p)
        new_s.append(s)
        new_cb.append(cb)
        new_fb.append(fb)
    return x, jnp.stack(new_s), jnp.stack(new_cb), jnp.stack(new_fb)


def kernel(x_prompt, x_sample, state_hgrn, state_conv, state_ffn_conv, c_prompt, c_sample, w_ada, b_ada, g_pre_mix, g_post_mix, g_pre_ffn, g_post_ffn, w_in, hgrn_lb_logits, hgrn_norm, w_pa, conv_w, w_pb, w_bgate, w_o, w_up, ffn_conv_w, w_down):
    depth, d = g_pre_mix.shape
    vec = lambda g: g.reshape(depth, 1, g.shape[-1])
    p = {
        "lb_logits": hgrn_lb_logits,
        "g_pre_mix": vec(g_pre_mix), "g_post_mix": vec(g_post_mix),
        "g_pre_ffn": vec(g_pre_ffn), "g_post_ffn": vec(g_post_ffn),
        "hgrn_norm": vec(hgrn_norm),
        "w_in": w_in.astype(BF), "w_pa": w_pa.astype(BF), "conv_w": conv_w, "w_pb": w_pb.astype(BF),
        "w_bgate": w_bgate.astype(BF), "w_o": w_o.astype(BF),
        "w_up": w_up.astype(BF), "ffn_conv_w": ffn_conv_w, "w_down": w_down.astype(BF),
    }
    bp = x_prompt.shape[0]
    bs = x_sample.shape[0]
    ada = _ada_call(jnp.concatenate([c_prompt, c_sample], axis=0), w_ada, b_ada)
    ada = ada.reshape(ada.shape[:3] + (1, d))

    s0 = jnp.zeros((depth, bp) + state_hgrn.shape[2:], F32)
    cb0 = jnp.zeros((depth, bp) + state_conv.shape[2:], F32)
    fb0 = jnp.zeros((depth, bp) + state_ffn_conv.shape[2:], F32)
    y_p, s_p, cb_p, fb_p = _trunk(x_prompt, ada, 0, s0, cb0, fb0, p)
    y_s, s_s, cb_s, fb_s = _trunk(x_sample, ada, bp, state_hgrn, state_conv, state_ffn_conv, p)
    return (y_p, y_s, s_p, cb_p, fb_p, s_s, cb_s, fb_s)
```

```python
import functools

import jax
import jax.numpy as jnp
from jax import lax
from jax.experimental import pallas as pl
from jax.experimental.pallas import tpu as pltpu

HEADS = 4
DK = 128
DV = 128
AW = HEADS * DV
CONV_HIST = 2
EPS = 1e-6
SUBLANES = 8
HIST_PAD = 8
HIST_LO = HIST_PAD - CONV_HIST
FFN_COLS = 256
SUB_ROWS = 256
SUBS_PER_STEP = 2
HGRN_CHUNK = 128
LOG2E = 1.4426950408889634
BF = jnp.bfloat16
F32 = jnp.float32
VMEM_LIMIT = 56 * 1024 * 1024


def _half_silu(hx):
    return hx + hx * jnp.tanh(hx)


def _rms_scale(x):
    return lax.rsqrt(jnp.mean(x * x, axis=-1, keepdims=True) + EPS)


def _dot(a, b):
    return jnp.dot(a, b, preferred_element_type=F32)


def _dot_nt(a, b):
    return lax.dot_general(a, b, (((1,), (1,)), ((), ())), preferred_element_type=F32)


def _dot_tn(a, b):
    return lax.dot_general(a, b, (((0,), (0,)), ((), ())), preferred_element_type=F32)


def _neg_abs(x):
    return lax.bitcast_convert_type(
        lax.bitcast_convert_type(x, jnp.uint32) | jnp.uint32(0x80000000), F32)


def _levels(chunk):
    lv, g = [], SUBLANES
    while 2 * g <= chunk:
        lv.append(g)
        g *= 2
    return lv


def _hgrn_pre(qa, lnk2, va, lf2):
    c = qa.shape[0]
    row = lax.broadcasted_iota(jnp.int32, (c, AW), 0)
    b2 = lf2
    d = 1
    while d < c:
        b2 = b2 + jnp.where(row >= d, pltpu.roll(b2, d, 0), 0.0)
        d *= 2
    c2 = b2 - lnk2
    blast = b2[c - 1:c, :]
    pre = {
        "blast": blast,
        "qc": (qa * jnp.exp2(b2)).astype(BF),
        "kc": jnp.exp2(blast - c2).astype(BF),
        "vb": va.astype(BF),
    }

    xs = []
    for g in _levels(c):
        ref_rows = [jnp.broadcast_to(b2[p * 2 * g + g - 1:p * 2 * g + g, :], (2 * g, AW))
                    for p in range(c // (2 * g))]
        rg = ref_rows[0] if len(ref_rows) == 1 else jnp.concatenate(ref_rows, axis=0)
        odd = (row & g) != 0
        eg = jnp.exp2(_neg_abs(jnp.where(odd, b2, c2) - rg))
        xs.append(jnp.where(odd, qa * eg, eg).astype(BF))
    pre["xs"] = xs

    nblk = c // SUBLANES
    b3 = b2.reshape(nblk, SUBLANES, AW)
    c3 = c2.reshape(nblk, SUBLANES, AW)
    q3 = qa.reshape(nblk, SUBLANES, AW)
    lane_s = lax.broadcasted_iota(jnp.int32, (c, c), 1) & (SUBLANES - 1)
    dtile = [jnp.zeros((c, c), F32) for _ in range(HEADS)]
    for s in range(SUBLANES):
        p = q3 * jnp.exp2(b3 - c3[:, s:s + 1, :])
        sel = lane_s == s
        for h in range(HEADS):
            a = jnp.sum(p[:, :, h * DK:(h + 1) * DK], axis=-1, keepdims=True).reshape(c, 1)
            dtile[h] = jnp.where(sel, a, dtile[h])
    pre["dtile"] = dtile
    return pre


def _hgrn_intra(pre):
    kc, vb, xs = pre.pop("kc"), pre.pop("vb"), pre.pop("xs")
    dtile = pre.pop("dtile")
    c = kc.shape[0]
    ti = lax.broadcasted_iota(jnp.int32, (c, c), 0)
    si = lax.broadcasted_iota(jnp.int32, (c, c), 1)
    txs = ti ^ si
    masks = [((ti & g) != 0) & (txs >= g) & (txs < 2 * g) for g in _levels(c)]
    diag_mask = (txs < SUBLANES) & (si <= ti)
    heads = [slice(h * DK, (h + 1) * DK) for h in range(HEADS)]
    amats = []
    for h, sl in enumerate(heads):
        amat = jnp.where(diag_mask, dtile[h], 0.0)
        for xg, m in zip(xs, masks):
            amat = jnp.where(m, _dot_nt(xg[:, sl], xg[:, sl]), amat)
        amats.append(amat.astype(BF))
    pre["inc"] = [_dot_tn(vb[:, sl], kc[:, sl]) for sl in heads]
    pre["o_intra"] = [_dot(amats[h], vb[:, sl]) for h, sl in enumerate(heads)]


def _hgrn_recur(pre, st):
    qc, blast = pre["qc"], pre["blast"]
    outs, new_st = [], []
    for h in range(HEADS):
        sl = slice(h * DK, (h + 1) * DK)
        outs.append(_dot_nt(qc[:, sl], st[h].astype(BF)) + pre["o_intra"][h])
        new_st.append(st[h] * jnp.exp2(blast[:, sl]) + pre["inc"][h])
    return jnp.concatenate(outs, axis=-1), new_st


def _conv_taps(ubuf_ref, u3, w, cols, row0):
    lt = u3.shape[1]
    ubuf_ref[:, HIST_PAD + row0:HIST_PAD + row0 + lt, cols] = u3
    return (ubuf_ref[:, HIST_LO + row0:HIST_LO + row0 + lt, cols] * w[0:1, :]
            + ubuf_ref[:, HIST_LO + 1 + row0:HIST_LO + 1 + row0 + lt, cols] * w[1:2, :]
            + u3 * w[2:3, :])


def _conv_carry(ubuf_ref, lt, cols):
    hist = ubuf_ref[:, lt + HIST_LO:lt + HIST_PAD, cols]
    ubuf_ref[:, HIST_LO:HIST_PAD, cols] = hist
    return hist


def _mixer_kernel(layer, nb, lt, sub, chunk,
                  x_ref, ada_ref, sin_ref, cbin_ref, lbl_ref, gpre_ref, gpost_ref, anorm_ref,
                  win_ref, wpa_ref, convw_ref, wpb_ref, wbg_ref, wo_ref,
                  xo_ref, sout_ref, cbout_ref, st_ref, ubuf_ref):
    t = pl.program_id(1)
    d = x_ref.shape[-1]
    cw = cbin_ref.shape[-1]
    sl = lt // sub
    r = nb * sl

    @pl.when(t == 0)
    def _():
        for bi in range(nb):
            for h in range(HEADS):
                st_ref[bi, h] = sin_ref[bi, h].T
        ubuf_ref[:, HIST_LO:HIST_PAD, :] = cbin_ref[...]

    ada = ada_ref[...]
    sh1, gt1 = ada[0], ada[2]
    pre_gain = gpre_ref[...] * (1.0 + ada[1])
    post_gain = gt1 * gpost_ref[...]
    if layer > 0:
        lg = lbl_ref[...]
        e = jnp.exp(lg - jnp.max(lg, axis=0, keepdims=True))
        pr = e / jnp.sum(e, axis=0, keepdims=True)
        lb = jnp.sum(pr[1:layer + 1, :], axis=0, keepdims=True)
        a0 = jnp.log(lb) * LOG2E
        c1 = jnp.log1p(-lb) * LOG2E
    state = {bi: [st_ref[bi, h] for h in range(HEADS)] for bi in range(nb)}
    ctx = [dict() for _ in range(sub)]

    def norm_in(k):
        c = ctx[k]
        c["x3"] = x_ref[:, k * sl:(k + 1) * sl, :]
        h3 = (c["x3"] * _rms_scale(c["x3"])) * pre_gain + sh1
        c["hb"] = h3.reshape(r, d).astype(BF)

    def proj_a(k):
        ctx[k]["proj"] = _dot(ctx[k]["hb"], win_ref[:, 0:4 * AW])

    def gates_pre(k):
        c = ctx[k]
        proj = c.pop("proj")
        hq, z, vi, hg = proj[:, 0:AW], proj[:, AW:2 * AW], proj[:, 2 * AW:3 * AW], proj[:, 3 * AW:4 * AW]
        qa = _half_silu(hq)
        z2 = z * LOG2E
        l1 = jnp.log(1.0 + jnp.exp2(_neg_abs(z2))) * LOG2E
        logsig2 = jnp.minimum(z2, 0.0) - l1
        lognsig2 = -jnp.maximum(z2, 0.0) - l1
        if layer == 0:
            lf2, lnk2 = logsig2, lognsig2
        else:
            y = c1 + logsig2
            lf2 = jnp.maximum(a0, y) + jnp.log(1.0 + jnp.exp2(_neg_abs(a0 - y))) * LOG2E
            lnk2 = c1 + lognsig2
        c["gs"] = _half_silu(hg) * jnp.concatenate([anorm_ref[...]] * HEADS, axis=-1)
        c["pre"] = []
        for bi in range(nb):
            for ci in range(sl // chunk):
                r0 = bi * sl + ci * chunk
                rs = slice(r0, r0 + chunk)
                c["pre"].append((bi, _hgrn_pre(qa[rs], lnk2[rs], vi[rs], lf2[rs])))

    def conv_branch(k):
        c = ctx[k]
        projb = _dot(c["hb"], win_ref[:, 4 * AW:7 * AW])
        bgt, cgt, xv = projb[:, 0:AW], projb[:, AW:2 * AW], projb[:, 2 * AW:3 * AW]
        u = _conv_taps(ubuf_ref, (cgt * xv).reshape(nb, sl, cw), convw_ref[...], slice(None), k * sl)
        c["y_b"] = _dot((bgt * u.reshape(r, cw)).astype(BF), wpb_ref[...])
        c["tg"] = jnp.tanh(_dot(c.pop("hb"), wbg_ref[...]))

    def intra(k):
        for _, pre in ctx[k]["pre"]:
            _hgrn_intra(pre)

    def recur(k):
        c = ctx[k]
        o_parts = []
        for bi, pre in c.pop("pre"):
            o, state[bi] = _hgrn_recur(pre, state[bi])
            o_parts.append(o)
        c["o"] = o_parts[0] if len(o_parts) == 1 else jnp.concatenate(o_parts, axis=0)

    def tail(k):
        c = ctx[k]
        o = c.pop("o")
        on = jnp.concatenate(
            [o[:, h * DV:(h + 1) * DV] * _rms_scale(o[:, h * DV:(h + 1) * DV]) for h in range(HEADS)],
            axis=-1) * c.pop("gs")
        y_a = _dot(on.astype(BF), wpa_ref[...])
        tg, y_b = c.pop("tg"), c.pop("y_b")
        mix = (y_a + tg[:, :d] * y_a) + (y_b + tg[:, d:] * y_b)
        m = _dot(mix.astype(BF), wo_ref[...])
        xo_ref[:, k * sl:(k + 1) * sl, :] = c.pop("x3") + (m * _rms_scale(m)).reshape(nb, sl, d) * post_gain

    if sub == 1:
        order = [(norm_in, 0), (proj_a, 0), (conv_branch, 0), (gates_pre, 0), (intra, 0), (recur, 0),
                 (tail, 0)]
    else:
        assert sub == 2
        order = [(norm_in, 0), (norm_in, 1), (proj_a, 0), (proj_a, 1), (gates_pre, 0), (conv_branch, 0),
                 (gates_pre, 1), (intra, 0), (recur, 0), (conv_branch, 1), (intra, 1), (tail, 0),
                 (recur, 1), (tail, 1)]
    for fn, k in order:
        fn(k)

    cbout_ref[...] = _conv_carry(ubuf_ref, lt, slice(None))
    for bi in range(nb):
        for h in range(HEADS):
            st_ref[bi, h] = state[bi][h]

    @pl.when(t == pl.num_programs(1) - 1)
    def _():
        for bi in range(nb):
            for h in range(HEADS):
                sout_ref[bi, h] = st_ref[bi, h].T


def _ffn_kernel(nb, lt, sub,
                x_ref, ada_ref, fbin_ref, gpre_ref, gpost_ref, wup_ref, convw_ref, wdn_ref,
                xo_ref, fbout_ref, ubuf_ref):
    d = x_ref.shape[-1]
    dff = fbin_ref.shape[-1]
    sl = lt // sub
    r = nb * sl
    cc = FFN_COLS
    nch = dff // cc

    @pl.when(pl.program_id(1) == 0)
    def _():
        ubuf_ref[:, HIST_LO:HIST_PAD, :] = fbin_ref[...] * 0.5

    ada = ada_ref[...]
    sh2 = ada[3]
    pre_gain = gpre_ref[...] * (1.0 + ada[4])
    post_gain = ada[5] * gpost_ref[...]
    x3s, hbs, ffs = {}, {}, {}

    def norm_in(k):
        x3s[k] = x_ref[:, k * sl:(k + 1) * sl, :]
        h3 = (x3s[k] * _rms_scale(x3s[k])) * pre_gain + sh2
        hbs[k] = h3.reshape(r, d).astype(BF)

    def up(k, j):
        c0 = j * cc
        return (_dot(hbs[k], wup_ref[:, c0:c0 + cc]), _dot(hbs[k], wup_ref[:, dff + c0:dff + c0 + cc]))

    def finish(k):
        ff = ffs.pop(k)
        xo_ref[:, k * sl:(k + 1) * sl, :] = x3s.pop(k) + (ff * _rms_scale(ff)).reshape(nb, sl, d) * post_gain

    items = [(k, j) for k in range(sub) for j in range(nch)]
    norm_in(0)
    nxt = up(0, 0)
    for i, (k, j) in enumerate(items):
        hgate, val = nxt
        if i + 1 < len(items):
            kn, jn = items[i + 1]
            if jn == 0:
                norm_in(kn)
            nxt = up(kn, jn)
        cols = slice(j * cc, (j + 1) * cc)
        hgc = _conv_taps(ubuf_ref, hgate.reshape(nb, sl, cc), convw_ref[:, cols], cols, k * sl)
        act = _half_silu(hgc.reshape(r, cc)) * val
        part = _dot(act.astype(BF), wdn_ref[cols, :])
        ffs[k] = part if j == 0 else ffs[k] + part
        if j == 0 and k > 0:
            finish(k - 1)
    finish(sub - 1)
    fbout_ref[...] = _conv_carry(ubuf_ref, lt, slice(None)) * 2.0


def _ada_kernel(c_ref, w_ref, b_ref, o_ref):
    c = c_ref[...]
    cs = (c / (1.0 + jnp.exp(-c))).astype(BF)
    o_ref[...] = _dot(cs, w_ref[...].astype(BF)) + b_ref[...]


def _ada_call(c_all, w_ada, b_ada):
    depth, d, d6 = w_ada.shape
    n = d6 // d
    bt = c_all.shape[0]
    return pl.pallas_call(
        _ada_kernel,
        grid=(depth, n),
        in_specs=[
            pl.BlockSpec((bt, d), lambda l, j: (0, 0)),
            pl.BlockSpec((None, d, d), lambda l, j: (l, 0, j)),
            pl.BlockSpec((None, None, 1, d), lambda l, j: (l, j, 0, 0)),
        ],
        out_specs=pl.BlockSpec((None, None, bt, d), lambda l, j: (l, j, 0, 0)),
        out_shape=jax.ShapeDtypeStruct((depth, n, bt, d), F32),
        compiler_params=pltpu.CompilerParams(dimension_semantics=("arbitrary", "arbitrary")),
        name="ada",
    )(c_all, w_ada, b_ada.reshape(depth, n, 1, d))


def _resident(shape, layer):
    nd = len(shape)
    return pl.BlockSpec((None,) + tuple(shape[1:]), lambda b, t: (layer,) + (0,) * (nd - 1),
                        pipeline_mode=pl.Buffered(1))


def _tile_plan(batch, length):
    if length % (SUB_ROWS * SUBS_PER_STEP) == 0:
        return 1, SUB_ROWS * SUBS_PER_STEP, SUBS_PER_STEP, HGRN_CHUNK
    assert length % SUBLANES == 0 and length <= SUB_ROWS
    return batch, length, 1, length


def _mixer_call(layer, x, ada, boff, s_in, cb_in, p):
    batch, length, d = x.shape
    nb, lt, sub, chunk = _tile_plan(batch, length)
    assert boff % nb == 0 and batch % nb == 0
    bo = boff // nb
    cw = cb_in.shape[-1]
    row = lambda b, t: (b, t, 0)
    per_b4 = lambda b, t: (b, 0, 0, 0)
    per_b3 = lambda b, t: (b, 0, 0)
    in_specs = [
        pl.BlockSpec((nb, lt, d), row),
        pl.BlockSpec((None, 6, nb, 1, d), lambda b, t: (layer, 0, b + bo, 0, 0)),
        pl.BlockSpec((nb, HEADS, DK, DV), per_b4),
        pl.BlockSpec((nb, CONV_HIST, cw), per_b3),
        pl.BlockSpec(p["lb_logits"].shape, lambda b, t: (0, 0)),
        _resident(p["g_pre_mix"].shape, layer),
        _resident(p["g_post_mix"].shape, layer),
        _resident(p["hgrn_norm"].shape, layer),
        _resident(p["w_in"].shape, layer),
        _resident(p["w_pa"].shape, layer),
        _resident(p["conv_w"].shape, layer),
        _resident(p["w_pb"].shape, layer),
        _resident(p["w_bgate"].shape, layer),
        _resident(p["w_o"].shape, layer),
    ]
    out_specs = [
        pl.BlockSpec((nb, lt, d), row),
        pl.BlockSpec((nb, HEADS, DK, DV), per_b4),
        pl.BlockSpec((nb, CONV_HIST, cw), per_b3),
    ]
    out_shape = [
        jax.ShapeDtypeStruct(x.shape, F32),
        jax.ShapeDtypeStruct(s_in.shape, F32),
        jax.ShapeDtypeStruct(cb_in.shape, F32),
    ]
    return pl.pallas_call(
        functools.partial(_mixer_kernel, layer, nb, lt, sub, chunk),
        grid=(batch // nb, length // lt),
        in_specs=in_specs,
        out_specs=out_specs,
        out_shape=out_shape,
        scratch_shapes=[
            pltpu.VMEM((nb, HEADS, DV, DK), F32),
            pltpu.VMEM((nb, lt + HIST_PAD, cw), F32),
        ],
        compiler_params=pltpu.CompilerParams(
            dimension_semantics=("arbitrary", "arbitrary"), vmem_limit_bytes=VMEM_LIMIT),
        name=f"mixer_l{layer}_n{nb}",
    )(x, ada, s_in, cb_in, p["lb_logits"], p["g_pre_mix"], p["g_post_mix"], p["hgrn_norm"],
      p["w_in"], p["w_pa"], p["conv_w"], p["w_pb"], p["w_bgate"], p["w_o"])


def _ffn_call(layer, x, ada, boff, fb_in, p):
    batch, length, d = x.shape
    nb, lt, sub, _ = _tile_plan(batch, length)
    bo = boff // nb
    dff = fb_in.shape[-1]
    assert dff % FFN_COLS == 0
    row = lambda b, t: (b, t, 0)
    per_b3 = lambda b, t: (b, 0, 0)
    in_specs = [
        pl.BlockSpec((nb, lt, d), row),
        pl.BlockSpec((None, 6, nb, 1, d), lambda b, t: (layer, 0, b + bo, 0, 0)),
        pl.BlockSpec((nb, CONV_HIST, dff), per_b3),
        _resident(p["g_pre_ffn"].shape, layer),
        _resident(p["g_post_ffn"].shape, layer),
        _resident(p["w_up"].shape, layer),
        _resident(p["ffn_conv_w"].shape, layer),
        _resident(p["w_down"].shape, layer),
    ]
    out_specs = [
        pl.BlockSpec((nb, lt, d), row),
        pl.BlockSpec((nb, CONV_HIST, dff), per_b3),
    ]
    out_shape = [
        jax.ShapeDtypeStruct(x.shape, F32),
        jax.ShapeDtypeStruct(fb_in.shape, F32),
    ]
    return pl.pallas_call(
        functools.partial(_ffn_kernel, nb, lt, sub),
        grid=(batch // nb, length // lt),
        in_specs=in_specs,
        out_specs=out_specs,
        out_shape=out_shape,
        scratch_shapes=[pltpu.VMEM((nb, lt + HIST_PAD, dff), F32)],
        compiler_params=pltpu.CompilerParams(
            dimension_semantics=("arbitrary", "arbitrary"), vmem_limit_bytes=VMEM_LIMIT),
        name=f"ffn_l{layer}_n{nb}",
    )(x, ada, fb_in, p["g_pre_ffn"], p["g_post_ffn"], p["w_up"], p["ffn_conv_w"], p["w_down"])


def _trunk(x, ada, boff, s_in, cb_in, fb_in, p):
    depth = p["w_in"].shape[0]
    new_s, new_cb, new_fb = [], [], []
    for l in range(depth):
        x, s, cb = _mixer_call(l, x, ada, boff, s_in[l], cb_in[l], p)
        x, fb = _ffn_call(l, x, ada, boff, fb_in[l], p)
        new_s.append(s)
        new_cb.append(cb)
        new_fb.append(fb)
    return x, jnp.stack(new_s), jnp.stack(new_cb), jnp.stack(new_fb)


def _halve_cols(w, ranges):
    scale = jnp.ones((w.shape[-1],), F32)
    for lo, hi in ranges:
        scale = scale.at[lo:hi].set(0.5)
    return (w * scale).astype(BF)


def kernel(x_prompt, x_sample, state_hgrn, state_conv, state_ffn_conv, c_prompt, c_sample, w_ada, b_ada, g_pre_mix, g_post_mix, g_pre_ffn, g_post_ffn, w_in, hgrn_lb_logits, hgrn_norm, w_pa, conv_w, w_pb, w_bgate, w_o, w_up, ffn_conv_w, w_down):
    depth, d = g_pre_mix.shape
    dff = w_down.shape[1]
    vec = lambda g: g.reshape(depth, 1, g.shape[-1])
    p = {
        "lb_logits": hgrn_lb_logits,
        "g_pre_mix": vec(g_pre_mix), "g_post_mix": vec(g_post_mix),
        "g_pre_ffn": vec(g_pre_ffn), "g_post_ffn": vec(g_post_ffn),
        "hgrn_norm": vec(hgrn_norm),
        "w_in": _halve_cols(w_in, [(0, AW), (3 * AW, 4 * AW)]),
        "w_pa": w_pa.astype(BF), "conv_w": conv_w, "w_pb": w_pb.astype(BF),
        "w_bgate": _halve_cols(w_bgate, [(0, w_bgate.shape[-1])]),
        "w_o": _halve_cols(w_o, [(0, w_o.shape[-1])]),
        "w_up": _halve_cols(w_up, [(0, dff)]), "ffn_conv_w": ffn_conv_w, "w_down": w_down.astype(BF),
    }
    bp = x_prompt.shape[0]
    ada = _ada_call(jnp.concatenate([c_prompt, c_sample], axis=0), w_ada, b_ada)
    ada = ada.reshape(ada.shape[:3] + (1, d))

    s0 = jnp.zeros((depth, bp) + state_hgrn.shape[2:], F32)
    cb0 = jnp.zeros((depth, bp) + state_conv.shape[2:], F32)
    fb0 = jnp.zeros((depth, bp) + state_ffn_conv.shape[2:], F32)
    y_p, s_p, cb_p, fb_p = _trunk(x_prompt, ada, 0, s0, cb0, fb0, p)
    y_s, s_s, cb_s, fb_s = _trunk(x_sample, ada, bp, state_hgrn, state_conv, state_ffn_conv, p)
    return (y_p, y_s, s_p, cb_p, fb_p, s_s, cb_s, fb_s)
```

```python
import functools

import jax
import jax.numpy as jnp
from jax import lax
from jax.experimental import pallas as pl
from jax.experimental.pallas import tpu as pltpu

HEADS = 4
DK = 128
DV = 128
AW = HEADS * DV
CONV_HIST = 2
EPS = 1e-6
SUBLANES = 8
DIAG = 8
HIST_PAD = 8
HIST_LO = HIST_PAD - CONV_HIST
FFN_COLS = 256
MIXER_SUB_ROWS = 512
MIXER_SUBS = 1
FFN_SUB_ROWS = 256
FFN_SUBS = 4
HGRN_CHUNK = 128
MIN_DOT_ROWS = 128
EDGE_PARTS = 2
LOG2E = 1.4426950408889634
BF = jnp.bfloat16
F32 = jnp.float32
VMEM_LIMIT = 60 * 1024 * 1024


def _half_silu(hx):
    return hx + hx * jnp.tanh(hx)


def _rms_scale(x):
    return lax.rsqrt(jnp.mean(x * x, axis=-1, keepdims=True) + EPS)


def _dot(a, b):
    return jnp.dot(a, b, preferred_element_type=F32)


def _dot_rows(a, b, parts):
    m = a.shape[0]
    if parts == 1 or m % (parts * MIN_DOT_ROWS):
        return _dot(a, b)
    step = m // parts
    return jnp.concatenate([_dot(a[i * step:(i + 1) * step], b) for i in range(parts)], axis=0)


def _dot_nt(a, b):
    return lax.dot_general(a, b, (((1,), (1,)), ((), ())), preferred_element_type=F32)


def _dot_tn(a, b):
    return lax.dot_general(a, b, (((0,), (0,)), ((), ())), preferred_element_type=F32)


def _neg_abs(x):
    return lax.bitcast_convert_type(
        lax.bitcast_convert_type(x, jnp.uint32) | jnp.uint32(0x80000000), F32)


def _levels(chunk):
    lv, g = [], DIAG
    while 2 * g <= chunk:
        lv.append(g)
        g *= 2
    return lv


def _hgrn_pre(qa, lnk2, va, lf2):
    c = qa.shape[0]
    row = lax.broadcasted_iota(jnp.int32, (c, AW), 0)
    b2 = lf2
    d = 1
    while d < c:
        b2 = b2 + jnp.where(row >= d, pltpu.roll(b2, d, 0), 0.0)
        d *= 2
    c2 = b2 - lnk2
    blast = b2[c - 1:c, :]
    pre = {
        "blast": blast,
        "qc": (qa * jnp.exp2(b2)).astype(BF),
        "kc": jnp.exp2(blast - c2).astype(BF),
        "vb": va.astype(BF),
    }

    nblk = c // SUBLANES
    b3 = b2.reshape(nblk, SUBLANES, AW)
    c3 = c2.reshape(nblk, SUBLANES, AW)
    q3 = qa.reshape(nblk, SUBLANES, AW)
    sub3 = lax.broadcasted_iota(jnp.int32, (nblk, SUBLANES, AW), 1)

    def rows_of_tile(x3, first, stride):
        out = x3[:, first:first + 1, :]
        for i in range(1, SUBLANES // stride):
            out = jnp.where(sub3 >= i * stride, x3[:, first + i * stride:first + i * stride + 1, :], out)
        return jnp.broadcast_to(out, (nblk, SUBLANES, AW))

    xs = []
    for g in _levels(c):
        if 2 * g <= SUBLANES:
            rg = rows_of_tile(b3, g - 1, 2 * g).reshape(c, AW)
        else:
            ref_rows = [jnp.broadcast_to(b2[p * 2 * g + g - 1:p * 2 * g + g, :], (2 * g, AW))
                        for p in range(c // (2 * g))]
            rg = ref_rows[0] if len(ref_rows) == 1 else jnp.concatenate(ref_rows, axis=0)
        odd = (row & g) != 0
        eg = jnp.exp2(_neg_abs(jnp.where(odd, b2, c2) - rg))
        xs.append(jnp.where(odd, qa * eg, eg).astype(BF))
    pre["xs"] = xs
    pre["diag_in"] = (b3, c3, q3)
    return pre


def _hgrn_diag(pre, zero_bits):
    b3, c3, q3 = pre.pop("diag_in")
    nblk = b3.shape[0]
    c = nblk * SUBLANES
    if zero_bits is not None:
        z3 = jnp.concatenate([zero_bits] * (AW // zero_bits.shape[1]), axis=1)[None]
        q3 = lax.bitcast_convert_type(lax.bitcast_convert_type(q3, jnp.uint32) | z3, F32)
    sub3 = lax.broadcasted_iota(jnp.int32, (nblk, SUBLANES, AW), 1)
    lane_s = lax.broadcasted_iota(jnp.int32, (c, c), 1) & (DIAG - 1)
    dtile = [jnp.zeros((c, c), F32) for _ in range(HEADS)]
    for s in range(DIAG):
        cs = c3[:, s:s + 1, :]
        for i in range(1, SUBLANES // DIAG):
            cs = jnp.where(sub3 >= i * DIAG, c3[:, s + i * DIAG:s + i * DIAG + 1, :], cs)
        p = q3 * jnp.exp2(b3 - cs)
        sel = lane_s == s
        for h in range(HEADS):
            a = jnp.sum(p[:, :, h * DK:(h + 1) * DK], axis=-1, keepdims=True).reshape(c, 1)
            dtile[h] = jnp.where(sel, a, dtile[h])
    pre["dtile"] = dtile


def _hgrn_intra(pre):
    kc, vb, xs = pre.pop("kc"), pre.pop("vb"), pre.pop("xs")
    dtile = pre.pop("dtile")
    c = kc.shape[0]
    ti = lax.broadcasted_iota(jnp.int32, (c, c), 0)
    si = lax.broadcasted_iota(jnp.int32, (c, c), 1)
    txs = ti ^ si
    masks = [((ti & g) != 0) & (txs >= g) & (txs < 2 * g) for g in _levels(c)]
    diag_mask = (txs < DIAG) & (si <= ti)
    heads = [slice(h * DK, (h + 1) * DK) for h in range(HEADS)]
    amats = []
    for h, sl in enumerate(heads):
        amat = jnp.where(diag_mask, dtile[h], 0.0)
        for xg, m in zip(xs, masks):
            amat = jnp.where(m, _dot_nt(xg[:, sl], xg[:, sl]), amat)
        amats.append(amat.astype(BF))
    pre["inc"] = [_dot_tn(vb[:, sl], kc[:, sl]) for sl in heads]
    pre["o_intra"] = [_dot(amats[h], vb[:, sl]) for h, sl in enumerate(heads)]


def _hgrn_recur(pre, st):
    qc, blast = pre["qc"], pre["blast"]
    outs, new_st = [], []
    for h in range(HEADS):
        sl = slice(h * DK, (h + 1) * DK)
        outs.append(_dot_nt(qc[:, sl], st[h].astype(BF)) + pre["o_intra"][h])
        new_st.append(st[h] * jnp.exp2(blast[:, sl]) + pre["inc"][h])
    return jnp.concatenate(outs, axis=-1), new_st


def _conv_taps(ubuf_ref, u3, w, cols, row0):
    lt = u3.shape[1]
    ubuf_ref[:, HIST_PAD + row0:HIST_PAD + row0 + lt, cols] = u3
    return (ubuf_ref[:, HIST_LO + row0:HIST_LO + row0 + lt, cols] * w[0:1, :]
            + ubuf_ref[:, HIST_LO + 1 + row0:HIST_LO + 1 + row0 + lt, cols] * w[1:2, :]
            + u3 * w[2:3, :])


def _conv_carry(ubuf_ref, lt, cols):
    hist = ubuf_ref[:, lt + HIST_LO:lt + HIST_PAD, cols]
    ubuf_ref[:, HIST_LO:HIST_PAD, cols] = hist
    return hist


def _mixer_kernel(layer, nb, lt, sub, chunk,
                  x_ref, ada_ref, sin_ref, cbin_ref, lbl_ref, gpre_ref, gpost_ref, anorm_ref,
                  win_ref, wpa_ref, convw_ref, wpb_ref, wbg_ref, wo_ref,
                  xo_ref, sout_ref, cbout_ref, st_ref, ubuf_ref, hb_ref):
    t = pl.program_id(1)
    d = x_ref.shape[-1]
    cw = cbin_ref.shape[-1]
    sl = lt // sub
    r = nb * sl

    @pl.when(t == 0)
    def _():
        for bi in range(nb):
            for h in range(HEADS):
                st_ref[bi, h] = sin_ref[bi, h].T
        ubuf_ref[:, HIST_LO:HIST_PAD, :] = cbin_ref[...]

    ada = ada_ref[...]
    sh1, gt1 = ada[0], ada[2]
    pre_gain = gpre_ref[...] * (1.0 + ada[1])
    post_gain = gt1 * gpost_ref[...]
    if layer > 0:
        lg = lbl_ref[...]
        e = jnp.exp(lg - jnp.max(lg, axis=0, keepdims=True))
        pr = e / jnp.sum(e, axis=0, keepdims=True)
        lb = jnp.sum(pr[1:layer + 1, :], axis=0, keepdims=True)
        a0 = jnp.log(lb) * LOG2E
        c1 = jnp.log1p(-lb) * LOG2E
    state = {bi: [st_ref[bi, h] for h in range(HEADS)] for bi in range(nb)}
    ctx = [dict() for _ in range(sub)]
    rt_zero = lax.bitcast_convert_type(
        jnp.full((SUBLANES, DK), lax.shift_right_arithmetic(t, jnp.int32(31)), jnp.int32), jnp.uint32)

    def norm_in(k):
        c = ctx[k]
        c["x3"] = x_ref[:, k * sl:(k + 1) * sl, :]
        h3 = (c["x3"] * _rms_scale(c["x3"])) * pre_gain + sh1
        hb_ref[k] = h3.reshape(r, d).astype(BF)

    def proj_a(k):
        ctx[k]["proj"] = _dot_rows(hb_ref[k], win_ref[:, 0:4 * AW], EDGE_PARTS if k == 0 else 1)

    def gates_pre(k):
        c = ctx[k]
        proj = c.pop("proj")
        hq, z, vi, hg = proj[:, 0:AW], proj[:, AW:2 * AW], proj[:, 2 * AW:3 * AW], proj[:, 3 * AW:4 * AW]
        qa = _half_silu(hq)
        z2 = z * LOG2E
        l1 = jnp.log(1.0 + jnp.exp2(_neg_abs(z2))) * LOG2E
        logsig2 = jnp.minimum(z2, 0.0) - l1
        lognsig2 = -jnp.maximum(z2, 0.0) - l1
        if layer == 0:
            lf2, lnk2 = logsig2, lognsig2
        else:
            y = c1 + logsig2
            lf2 = jnp.maximum(a0, y) + jnp.log(1.0 + jnp.exp2(_neg_abs(a0 - y))) * LOG2E
            lnk2 = c1 + lognsig2
        c["gs"] = _half_silu(hg) * jnp.concatenate([anorm_ref[...]] * HEADS, axis=-1)
        c["pre"] = []
        for bi in range(nb):
            for ci in range(sl // chunk):
                r0 = bi * sl + ci * chunk
                rs = slice(r0, r0 + chunk)
                c["pre"].append((bi, _hgrn_pre(qa[rs], lnk2[rs], vi[rs], lf2[rs])))

    def conv_branch(k):
        c = ctx[k]
        projb = _dot(hb_ref[k], win_ref[:, 4 * AW:7 * AW])
        bgt, cgt, xv = projb[:, 0:AW], projb[:, AW:2 * AW], projb[:, 2 * AW:3 * AW]
        u = _conv_taps(ubuf_ref, (cgt * xv).reshape(nb, sl, cw), convw_ref[...], slice(None), k * sl)
        c["y_b"] = _dot((bgt * u.reshape(r, cw)).astype(BF), wpb_ref[...])
        c["tg"] = jnp.tanh(_dot(hb_ref[k], wbg_ref[...]))

    def diag(k):
        tg = ctx[k]["tg"]
        zero_bits = lax.bitcast_convert_type(tg[0:SUBLANES, 0:DK], jnp.uint32) & rt_zero
        for _, pre in ctx[k]["pre"]:
            _hgrn_diag(pre, zero_bits)

    def intra(k):
        for _, pre in ctx[k]["pre"]:
            _hgrn_intra(pre)

    def recur(k):
        c = ctx[k]
        o_parts = []
        for bi, pre in c.pop("pre"):
            o, state[bi] = _hgrn_recur(pre, state[bi])
            o_parts.append(o)
        c["o"] = o_parts[0] if len(o_parts) == 1 else jnp.concatenate(o_parts, axis=0)

    def tail(k):
        c = ctx[k]
        o = c.pop("o")
        on = jnp.concatenate(
            [o[:, h * DV:(h + 1) * DV] * _rms_scale(o[:, h * DV:(h + 1) * DV]) for h in range(HEADS)],
            axis=-1) * c.pop("gs")
        y_a = _dot(on.astype(BF), wpa_ref[...])
        tg, y_b = c.pop("tg"), c.pop("y_b")
        mix = (y_a + tg[:, :d] * y_a) + (y_b + tg[:, d:] * y_b)
        m = _dot_rows(mix.astype(BF), wo_ref[...], EDGE_PARTS if k == sub - 1 else 1)
        xo_ref[:, k * sl:(k + 1) * sl, :] = c.pop("x3") + (m * _rms_scale(m)).reshape(nb, sl, d) * post_gain

    stages = [norm_in, proj_a, gates_pre, conv_branch, diag, intra, recur, tail]
    for step in range(len(stages) + sub - 1):
        for k in range(sub):
            if 0 <= step - k < len(stages):
                stages[step - k](k)

    cbout_ref[...] = _conv_carry(ubuf_ref, lt, slice(None))
    for bi in range(nb):
        for h in range(HEADS):
            st_ref[bi, h] = state[bi][h]

    @pl.when(t == pl.num_programs(1) - 1)
    def _():
        for bi in range(nb):
            for h in range(HEADS):
                sout_ref[bi, h] = st_ref[bi, h].T


def _ffn_kernel(nb, lt, sub,
                x_ref, ada_ref, fbin_ref, gpre_ref, gpost_ref, wup_ref, convw_ref, wdn_ref,
                xo_ref, fbout_ref, ubuf_ref, hb_ref):
    d = x_ref.shape[-1]
    dff = fbin_ref.shape[-1]
    sl = lt // sub
    r = nb * sl
    cc = FFN_COLS
    nch = dff // cc

    @pl.when(pl.program_id(1) == 0)
    def _():
        ubuf_ref[:, HIST_LO:HIST_PAD, :] = fbin_ref[...] * 0.5

    ada = ada_ref[...]
    sh2 = ada[3]
    pre_gain = gpre_ref[...] * (1.0 + ada[4])
    post_gain = ada[5] * gpost_ref[...]
    x3s, hbs, ffs = {}, {}, {}

    def norm_in(k):
        x3s[k] = x_ref[:, k * sl:(k + 1) * sl, :]
        h3 = (x3s[k] * _rms_scale(x3s[k])) * pre_gain + sh2
        hb_ref[k] = h3.reshape(r, d).astype(BF)
        hbs[k] = hb_ref.at[k]

    def up(k, j):
        c0 = j * cc
        parts = EDGE_PARTS if (k, j) == (0, 0) else 1
        return (_dot_rows(hbs[k][...], wup_ref[:, c0:c0 + cc], parts),
                _dot_rows(hbs[k][...], wup_ref[:, dff + c0:dff + c0 + cc], parts))

    def finish(k):
        ff = ffs.pop(k)
        xo_ref[:, k * sl:(k + 1) * sl, :] = x3s.pop(k) + (ff * _rms_scale(ff)).reshape(nb, sl, d) * post_gain

    items = [(k, j) for k in range(sub) for j in range(nch)]
    norm_in(0)
    nxt = up(0, 0)
    for i, (k, j) in enumerate(items):
        hgate, val = nxt
        if i + 1 < len(items):
            kn, jn = items[i + 1]
            if jn == 0:
                norm_in(kn)
            nxt = up(kn, jn)
        cols = slice(j * cc, (j + 1) * cc)
        hgc = _conv_taps(ubuf_ref, hgate.reshape(nb, sl, cc), convw_ref[:, cols], cols, k * sl)
        act = _half_silu(hgc.reshape(r, cc)) * val
        part = _dot_rows(act.astype(BF), wdn_ref[cols, :], EDGE_PARTS if i + 1 == len(items) else 1)
        ffs[k] = part if j == 0 else ffs[k] + part
        if j == 0 and k > 0:
            finish(k - 1)
    finish(sub - 1)
    fbout_ref[...] = _conv_carry(ubuf_ref, lt, slice(None)) * 2.0


def _ada_kernel(c_ref, w_ref, b_ref, o_ref):
    c = c_ref[...]
    cs = (c / (1.0 + jnp.exp(-c))).astype(BF)
    o_ref[...] = _dot(cs, w_ref[...].astype(BF)) + b_ref[...]


def _ada_call(c_all, w_ada, b_ada):
    depth, d, d6 = w_ada.shape
    n = d6 // d
    bt = c_all.shape[0]
    return pl.pallas_call(
        _ada_kernel,
        grid=(depth, n),
        in_specs=[
            pl.BlockSpec((bt, d), lambda l, j: (0, 0)),
            pl.BlockSpec((None, d, d), lambda l, j: (l, 0, j)),
            pl.BlockSpec((None, None, 1, d), lambda l, j: (l, j, 0, 0)),
        ],
        out_specs=pl.BlockSpec((None, None, bt, d), lambda l, j: (l, j, 0, 0)),
        out_shape=jax.ShapeDtypeStruct((depth, n, bt, d), F32),
        compiler_params=pltpu.CompilerParams(dimension_semantics=("arbitrary", "arbitrary")),
        name="ada",
    )(c_all, w_ada, b_ada.reshape(depth, n, 1, d))


def _resident(shape, layer):
    nd = len(shape)
    return pl.BlockSpec((None,) + tuple(shape[1:]), lambda b, t: (layer,) + (0,) * (nd - 1),
                        pipeline_mode=pl.Buffered(1))


def _tile_plan(batch, length, sub_rows, subs):
    if length % (sub_rows * subs) == 0:
        return 1, sub_rows * subs, subs, HGRN_CHUNK
    assert length % SUBLANES == 0 and length <= sub_rows
    return batch, length, 1, length


def _mixer_call(layer, x, ada, boff, s_in, cb_in, p):
    batch, length, d = x.shape
    nb, lt, sub, chunk = _tile_plan(batch, length, MIXER_SUB_ROWS, MIXER_SUBS)
    assert boff % nb == 0 and batch % nb == 0
    bo = boff // nb
    cw = cb_in.shape[-1]
    row = lambda b, t: (b, t, 0)
    per_b4 = lambda b, t: (b, 0, 0, 0)
    per_b3 = lambda b, t: (b, 0, 0)
    in_specs = [
        pl.BlockSpec((nb, lt, d), row),
        pl.BlockSpec((None, 6, nb, 1, d), lambda b, t: (layer, 0, b + bo, 0, 0)),
        pl.BlockSpec((nb, HEADS, DK, DV), per_b4),
        pl.BlockSpec((nb, CONV_HIST, cw), per_b3),
        pl.BlockSpec(p["lb_logits"].shape, lambda b, t: (0, 0)),
        _resident(p["g_pre_mix"].shape, layer),
        _resident(p["g_post_mix"].shape, layer),
        _resident(p["hgrn_norm"].shape, layer),
        _resident(p["w_in"].shape, layer),
        _resident(p["w_pa"].shape, layer),
        _resident(p["conv_w"].shape, layer),
        _resident(p["w_pb"].shape, layer),
        _resident(p["w_bgate"].shape, layer),
        _resident(p["w_o"].shape, layer),
    ]
    out_specs = [
        pl.BlockSpec((nb, lt, d), row),
        pl.BlockSpec((nb, HEADS, DK, DV), per_b4),
        pl.BlockSpec((nb, CONV_HIST, cw), per_b3),
    ]
    out_shape = [
        jax.ShapeDtypeStruct(x.shape, F32),
        jax.ShapeDtypeStruct(s_in.shape, F32),
        jax.ShapeDtypeStruct(cb_in.shape, F32),
    ]
    return pl.pallas_call(
        functools.partial(_mixer_kernel, layer, nb, lt, sub, chunk),
        grid=(batch // nb, length // lt),
        in_specs=in_specs,
        out_specs=out_specs,
        out_shape=out_shape,
        scratch_shapes=[
            pltpu.VMEM((nb, HEADS, DV, DK), F32),
            pltpu.VMEM((nb, lt + HIST_PAD, cw), F32),
            pltpu.VMEM((sub, nb * lt // sub, d), BF),
        ],
        compiler_params=pltpu.CompilerParams(
            dimension_semantics=("arbitrary", "arbitrary"), vmem_limit_bytes=VMEM_LIMIT),
        name=f"mixer_l{layer}_n{nb}",
    )(x, ada, s_in, cb_in, p["lb_logits"], p["g_pre_mix"], p["g_post_mix"], p["hgrn_norm"],
      p["w_in"], p["w_pa"], p["conv_w"], p["w_pb"], p["w_bgate"], p["w_o"])


def _ffn_call(layer, x, ada, boff, fb_in, p):
    batch, length, d = x.shape
    nb, lt, sub, _ = _tile_plan(batch, length, FFN_SUB_ROWS, FFN_SUBS)
    bo = boff // nb
    dff = fb_in.shape[-1]
    assert dff % FFN_COLS == 0
    row = lambda b, t: (b, t, 0)
    per_b3 = lambda b, t: (b, 0, 0)
    in_specs = [
        pl.BlockSpec((nb, lt, d), row),
        pl.BlockSpec((None, 6, nb, 1, d), lambda b, t: (layer, 0, b + bo, 0, 0)),
        pl.BlockSpec((nb, CONV_HIST, dff), per_b3),
        _resident(p["g_pre_ffn"].shape, layer),
        _resident(p["g_post_ffn"].shape, layer),
        _resident(p["w_up"].shape, layer),
        _resident(p["ffn_conv_w"].shape, layer),
        _resident(p["w_down"].shape, layer),
    ]
    out_specs = [
        pl.BlockSpec((nb, lt, d), row),
        pl.BlockSpec((nb, CONV_HIST, dff), per_b3),
    ]
    out_shape = [
        jax.ShapeDtypeStruct(x.shape, F32),
        jax.ShapeDtypeStruct(fb_in.shape, F32),
    ]
    return pl.pallas_call(
        functools.partial(_ffn_kernel, nb, lt, sub),
        grid=(batch // nb, length // lt),
        in_specs=in_specs,
        out_specs=out_specs,
        out_shape=out_shape,
        scratch_shapes=[pltpu.VMEM((nb, lt + HIST_PAD, dff), F32),
                        pltpu.VMEM((sub, nb * lt // sub, d), BF)],
        compiler_params=pltpu.CompilerParams(
            dimension_semantics=("arbitrary", "arbitrary"), vmem_limit_bytes=VMEM_LIMIT),
        name=f"ffn_l{layer}_n{nb}",
    )(x, ada, fb_in, p["g_pre_ffn"], p["g_post_ffn"], p["w_up"], p["ffn_conv_w"], p["w_down"])


def _trunk(x, ada, boff, s_in, cb_in, fb_in, p):
    depth = p["w_in"].shape[0]
    new_s, new_cb, new_fb = [], [], []
    for l in range(depth):
        x, s, cb = _mixer_call(l, x, ada, boff, s_in[l], cb_in[l], p)
        x, fb = _ffn_call(l, x, ada, boff, fb_in[l], p)
        new_s.append(s)
        new_cb.append(cb)
        new_fb.append(fb)
    return x, jnp.stack(new_s), jnp.stack(new_cb), jnp.stack(new_fb)


def _halve_cols(w, ranges):
    scale = jnp.ones((w.shape[-1],), F32)
    for lo, hi in ranges:
        scale = scale.at[lo:hi].set(0.5)
    return (w * scale).astype(BF)


def kernel(x_prompt, x_sample, state_hgrn, state_conv, state_ffn_conv, c_prompt, c_sample, w_ada, b_ada, g_pre_mix, g_post_mix, g_pre_ffn, g_post_ffn, w_in, hgrn_lb_logits, hgrn_norm, w_pa, conv_w, w_pb, w_bgate, w_o, w_up, ffn_conv_w, w_down):
    depth, d = g_pre_mix.shape
    dff = w_down.shape[1]
    vec = lambda g: g.reshape(depth, 1, g.shape[-1])
    p = {
        "lb_logits": hgrn_lb_logits,
        "g_pre_mix": vec(g_pre_mix), "g_post_mix": vec(g_post_mix),
        "g_pre_ffn": vec(g_pre_ffn), "g_post_ffn": vec(g_post_ffn),
        "hgrn_norm": vec(hgrn_norm),
        "w_in": _halve_cols(w_in, [(0, AW), (3 * AW, 4 * AW)]),
        "w_pa": w_pa.astype(BF), "conv_w": conv_w, "w_pb": w_pb.astype(BF),
        "w_bgate": _halve_cols(w_bgate, [(0, w_bgate.shape[-1])]),
        "w_o": _halve_cols(w_o, [(0, w_o.shape[-1])]),
        "w_up": _halve_cols(w_up, [(0, dff)]), "ffn_conv_w": ffn_conv_w, "w_down": w_down.astype(BF),
    }
    bp = x_prompt.shape[0]
    ada = _ada_call(jnp.concatenate([c_prompt, c_sample], axis=0), w_ada, b_ada)
    ada = ada.reshape(ada.shape[:3] + (1, d))

    s0 = jnp.zeros((depth, bp) + state_hgrn.shape[2:], F32)
    cb0 = jnp.zeros((depth, bp) + state_conv.shape[2:], F32)
    fb0 = jnp.zeros((depth, bp) + state_ffn_conv.shape[2:], F32)
    y_p, s_p, cb_p, fb_p = _trunk(x_prompt, ada, 0, s0, cb0, fb0, p)
    y_s, s_s, cb_s, fb_s = _trunk(x_sample, ada, bp, state_hgrn, state_conv, state_ffn_conv, p)
    return (y_p, y_s, s_p, cb_p, fb_p, s_s, cb_s, fb_s)
```

```python
import functools

import jax
import jax.numpy as jnp
from jax import lax
from jax.experimental import pallas as pl
from jax.experimental.pallas import tpu as pltpu

HEADS = 4
DK = 128
DV = 128
AW = HEADS * DV
CONV_HIST = 2
EPS = 1e-6
SUBLANES = 8
DIAG = 8
HIST_PAD = 8
HIST_LO = HIST_PAD - CONV_HIST
FFN_COLS = 256
MIXER_SUB_ROWS = 512
MIXER_SUBS = 1
FFN_SUB_ROWS = 256
FFN_SUBS = 4
HGRN_CHUNK = 128
MIN_DOT_ROWS = 128
EDGE_PARTS = 2
LOG2E = 1.4426950408889634
BF = jnp.bfloat16
F32 = jnp.float32
VMEM_LIMIT = 60 * 1024 * 1024


def _half_silu(hx):
    return hx + hx * jnp.tanh(hx)


def _rms_scale(x):
    return lax.rsqrt(jnp.mean(x * x, axis=-1, keepdims=True) + EPS)


def _dot(a, b):
    return jnp.dot(a, b, preferred_element_type=F32)


def _dot_rows(a, b, parts):
    m = a.shape[0]
    if parts == 1 or m % (parts * MIN_DOT_ROWS):
        return _dot(a, b)
    step = m // parts
    return jnp.concatenate([_dot(a[i * step:(i + 1) * step], b) for i in range(parts)], axis=0)


def _dot_nt(a, b):
    return lax.dot_general(a, b, (((1,), (1,)), ((), ())), preferred_element_type=F32)


def _dot_tn(a, b):
    return lax.dot_general(a, b, (((0,), (0,)), ((), ())), preferred_element_type=F32)


def _neg_abs(x):
    return lax.bitcast_convert_type(
        lax.bitcast_convert_type(x, jnp.uint32) | jnp.uint32(0x80000000), F32)


def _levels(chunk):
    lv, g = [], DIAG
    while 2 * g <= chunk:
        lv.append(g)
        g *= 2
    return lv


def _hgrn_pre(qa, lnk2, va, lf2):
    c = qa.shape[0]
    row = lax.broadcasted_iota(jnp.int32, (c, AW), 0)
    b2 = lf2
    d = 1
    while d < c:
        b2 = b2 + jnp.where(row >= d, pltpu.roll(b2, d, 0), 0.0)
        d *= 2
    c2 = b2 - lnk2
    blast = b2[c - 1:c, :]
    pre = {
        "blast": blast,
        "qc": (qa * jnp.exp2(b2)).astype(BF),
        "kc": jnp.exp2(blast - c2).astype(BF),
        "vb": va.astype(BF),
    }

    nblk = c // SUBLANES
    b3 = b2.reshape(nblk, SUBLANES, AW)
    c3 = c2.reshape(nblk, SUBLANES, AW)
    q3 = qa.reshape(nblk, SUBLANES, AW)
    sub3 = lax.broadcasted_iota(jnp.int32, (nblk, SUBLANES, AW), 1)

    def rows_of_tile(x3, first, stride):
        out = x3[:, first:first + 1, :]
        for i in range(1, SUBLANES // stride):
            out = jnp.where(sub3 >= i * stride, x3[:, first + i * stride:first + i * stride + 1, :], out)
        return jnp.broadcast_to(out, (nblk, SUBLANES, AW))

    xs = []
    for g in _levels(c):
        if 2 * g <= SUBLANES:
            rg = rows_of_tile(b3, g - 1, 2 * g).reshape(c, AW)
        else:
            ref_rows = [jnp.broadcast_to(b2[p * 2 * g + g - 1:p * 2 * g + g, :], (2 * g, AW))
                        for p in range(c // (2 * g))]
            rg = ref_rows[0] if len(ref_rows) == 1 else jnp.concatenate(ref_rows, axis=0)
        odd = (row & g) != 0
        eg = jnp.exp2(_neg_abs(jnp.where(odd, b2, c2) - rg))
        xs.append(jnp.where(odd, qa * eg, eg).astype(BF))
    pre["xs"] = xs
    pre["diag_in"] = (b3, c3, q3)
    return pre


def _hgrn_diag(pre, zero_bits):
    b3, c3, q3 = pre.pop("diag_in")
    nblk = b3.shape[0]
    c = nblk * SUBLANES
    if zero_bits is not None:
        z3 = jnp.concatenate([zero_bits] * (AW // zero_bits.shape[1]), axis=1)[None]
        q3 = lax.bitcast_convert_type(lax.bitcast_convert_type(q3, jnp.uint32) | z3, F32)
    sub3 = lax.broadcasted_iota(jnp.int32, (nblk, SUBLANES, AW), 1)
    lane_s = lax.broadcasted_iota(jnp.int32, (c, c), 1) & (DIAG - 1)
    dtile = [jnp.zeros((c, c), F32) for _ in range(HEADS)]
    for s in range(DIAG):
        cs = c3[:, s:s + 1, :]
        for i in range(1, SUBLANES // DIAG):
            cs = jnp.where(sub3 >= i * DIAG, c3[:, s + i * DIAG:s + i * DIAG + 1, :], cs)
        p = q3 * jnp.exp2(b3 - cs)
        sel = lane_s == s
        for h in range(HEADS):
            a = jnp.sum(p[:, :, h * DK:(h + 1) * DK], axis=-1, keepdims=True).reshape(c, 1)
            dtile[h] = jnp.where(sel, a, dtile[h])
    pre["dtile"] = dtile


def _hgrn_intra(pre):
    kc, vb, xs = pre.pop("kc"), pre.pop("vb"), pre.pop("xs")
    dtile = pre.pop("dtile")
    c = kc.shape[0]
    ti = lax.broadcasted_iota(jnp.int32, (c, c), 0)
    si = lax.broadcasted_iota(jnp.int32, (c, c), 1)
    txs = ti ^ si
    masks = [((ti & g) != 0) & (txs >= g) & (txs < 2 * g) for g in _levels(c)]
    diag_mask = (txs < DIAG) & (si <= ti)
    heads = [slice(h * DK, (h + 1) * DK) for h in range(HEADS)]
    qc, blast = pre.pop("qc"), pre.pop("blast")
    lhs = []
    for h, sl in enumerate(heads):
        amat = jnp.where(diag_mask, dtile[h], 0.0)
        for xg, m in zip(xs, masks):
            amat = jnp.where(m, _dot_nt(xg[:, sl], xg[:, sl]), amat)
        lhs.append(jnp.concatenate([qc[:, sl], amat.astype(BF)], axis=1))
    pre["lhs"] = lhs
    pre["vb"] = [vb[:, sl] for sl in heads]
    pre["inc"] = [_dot_tn(kc[:, sl], vb[:, sl]) for sl in heads]
    pre["dec"] = [jnp.exp2(jnp.broadcast_to(blast[:, sl], (DV, DK)).T) for sl in heads]


def _hgrn_recur(pre, st):
    outs, new_st = [], []
    for h in range(HEADS):
        rhs = jnp.concatenate([st[h].astype(BF), pre["vb"][h]], axis=0)
        outs.append(_dot(pre["lhs"][h], rhs))
        new_st.append(st[h] * pre["dec"][h] + pre["inc"][h])
    return jnp.concatenate(outs, axis=-1), new_st


def _conv_taps(ubuf_ref, u3, w, cols, row0):
    lt = u3.shape[1]
    ubuf_ref[:, HIST_PAD + row0:HIST_PAD + row0 + lt, cols] = u3
    return (ubuf_ref[:, HIST_LO + row0:HIST_LO + row0 + lt, cols] * w[0:1, :]
            + ubuf_ref[:, HIST_LO + 1 + row0:HIST_LO + 1 + row0 + lt, cols] * w[1:2, :]
            + u3 * w[2:3, :])


def _conv_carry(ubuf_ref, lt, cols):
    hist = ubuf_ref[:, lt + HIST_LO:lt + HIST_PAD, cols]
    ubuf_ref[:, HIST_LO:HIST_PAD, cols] = hist
    return hist


def _mixer_kernel(layer, nb, lt, sub, chunk,
                  x_ref, ada_ref, sin_ref, cbin_ref, lbl_ref, gpre_ref, gpost_ref, anorm_ref,
                  win_ref, wpa_ref, convw_ref, wpb_ref, wbg_ref, wo_ref,
                  xo_ref, sout_ref, cbout_ref, st_ref, ubuf_ref, hb_ref):
    t = pl.program_id(1)
    d = x_ref.shape[-1]
    cw = cbin_ref.shape[-1]
    sl = lt // sub
    r = nb * sl

    @pl.when(t == 0)
    def _():
        for bi in range(nb):
            for h in range(HEADS):
                st_ref[bi, h] = sin_ref[bi, h]
        ubuf_ref[:, HIST_LO:HIST_PAD, :] = cbin_ref[...]

    ada = ada_ref[...]
    sh1, gt1 = ada[0], ada[2]
    pre_gain = gpre_ref[...] * (1.0 + ada[1])
    post_gain = gt1 * gpost_ref[...]
    if layer > 0:
        lg = lbl_ref[...]
        e = jnp.exp(lg - jnp.max(lg, axis=0, keepdims=True))
        pr = e / jnp.sum(e, axis=0, keepdims=True)
        lb = jnp.sum(pr[1:layer + 1, :], axis=0, keepdims=True)
        a0 = jnp.log(lb) * LOG2E
        c1 = jnp.log1p(-lb) * LOG2E
    state = {bi: [st_ref[bi, h] for h in range(HEADS)] for bi in range(nb)}
    ctx = [dict() for _ in range(sub)]
    rt_zero = lax.bitcast_convert_type(
        jnp.full((SUBLANES, DK), lax.shift_right_arithmetic(t, jnp.int32(31)), jnp.int32), jnp.uint32)

    def norm_in(k):
        c = ctx[k]
        c["x3"] = x_ref[:, k * sl:(k + 1) * sl, :]
        h3 = (c["x3"] * _rms_scale(c["x3"])) * pre_gain + sh1
        hb_ref[k] = h3.reshape(r, d).astype(BF)

    def proj_a(k):
        ctx[k]["proj"] = _dot_rows(hb_ref[k], win_ref[:, 0:4 * AW], EDGE_PARTS if k == 0 else 1)

    def gates_pre(k):
        c = ctx[k]
        proj = c.pop("proj")
        hq, z, vi, hg = proj[:, 0:AW], proj[:, AW:2 * AW], proj[:, 2 * AW:3 * AW], proj[:, 3 * AW:4 * AW]
        qa = _half_silu(hq)
        z2 = z * LOG2E
        l1 = jnp.log(1.0 + jnp.exp2(_neg_abs(z2))) * LOG2E
        logsig2 = jnp.minimum(z2, 0.0) - l1
        lognsig2 = -jnp.maximum(z2, 0.0) - l1
        if layer == 0:
            lf2, lnk2 = logsig2, lognsig2
        else:
            y = c1 + logsig2
            lf2 = jnp.maximum(a0, y) + jnp.log(1.0 + jnp.exp2(_neg_abs(a0 - y))) * LOG2E
            lnk2 = c1 + lognsig2
        c["gs"] = _half_silu(hg) * jnp.concatenate([anorm_ref[...]] * HEADS, axis=-1)
        c["pre"] = []
        for bi in range(nb):
            for ci in range(sl // chunk):
                r0 = bi * sl + ci * chunk
                rs = slice(r0, r0 + chunk)
                c["pre"].append((bi, _hgrn_pre(qa[rs], lnk2[rs], vi[rs], lf2[rs])))

    def conv_branch(k):
        c = ctx[k]
        projb = _dot(hb_ref[k], win_ref[:, 4 * AW:7 * AW])
        bgt, cgt, xv = projb[:, 0:AW], projb[:, AW:2 * AW], projb[:, 2 * AW:3 * AW]
        u = _conv_taps(ubuf_ref, (cgt * xv).reshape(nb, sl, cw), convw_ref[...], slice(None), k * sl)
        c["y_b"] = _dot((bgt * u.reshape(r, cw)).astype(BF), wpb_ref[...])

    def diag(k):
        y_b = ctx[k]["y_b"]
        zero_bits = lax.bitcast_convert_type(y_b[0:SUBLANES, 0:DK], jnp.uint32) & rt_zero
        for _, pre in ctx[k]["pre"]:
            _hgrn_diag(pre, zero_bits)

    def gate_mm(k):
        c = ctx[k]
        probe = pltpu.bitcast(c["pre"][0][1]["xs"][0][0:2 * SUBLANES, 0:DK], jnp.uint32) & rt_zero
        hb32 = pltpu.bitcast(hb_ref[k], jnp.uint32)
        rows32 = hb32.shape[0]
        hb32 = hb32.reshape(rows32 // SUBLANES, SUBLANES, d) | jnp.concatenate([probe] * (d // DK), axis=1)[None]
        hb = pltpu.bitcast(hb32.reshape(rows32, d), BF)
        c["tg"] = jnp.tanh(_dot(hb, wbg_ref[...]))

    def intra(k):
        for _, pre in ctx[k]["pre"]:
            _hgrn_intra(pre)

    def recur(k):
        c = ctx[k]
        o_parts = []
        for bi, pre in c.pop("pre"):
            o, state[bi] = _hgrn_recur(pre, state[bi])
            o_parts.append(o)
        c["o"] = o_parts[0] if len(o_parts) == 1 else jnp.concatenate(o_parts, axis=0)

    def tail(k):
        c = ctx[k]
        o = c.pop("o")
        on = jnp.concatenate(
            [o[:, h * DV:(h + 1) * DV] * _rms_scale(o[:, h * DV:(h + 1) * DV]) for h in range(HEADS)],
            axis=-1) * c.pop("gs")
        y_a = _dot(on.astype(BF), wpa_ref[...])
        tg, y_b = c.pop("tg"), c.pop("y_b")
        mix = (y_a + tg[:, :d] * y_a) + (y_b + tg[:, d:] * y_b)
        m = _dot_rows(mix.astype(BF), wo_ref[...], EDGE_PARTS if k == sub - 1 else 1)
        xo_ref[:, k * sl:(k + 1) * sl, :] = c.pop("x3") + (m * _rms_scale(m)).reshape(nb, sl, d) * post_gain

    stages = [norm_in, proj_a, gates_pre, conv_branch, diag, gate_mm, intra, recur, tail]
    for step in range(len(stages) + sub - 1):
        for k in range(sub):
            if 0 <= step - k < len(stages):
                stages[step - k](k)

    cbout_ref[...] = _conv_carry(ubuf_ref, lt, slice(None))
    for bi in range(nb):
        for h in range(HEADS):
            st_ref[bi, h] = state[bi][h]

    @pl.when(t == pl.num_programs(1) - 1)
    def _():
        for bi in range(nb):
            for h in range(HEADS):
                sout_ref[bi, h] = st_ref[bi, h]


def _ffn_kernel(nb, lt, sub,
                x_ref, ada_ref, fbin_ref, gpre_ref, gpost_ref, wup_ref, convw_ref, wdn_ref,
                xo_ref, fbout_ref, ubuf_ref, hb_ref):
    d = x_ref.shape[-1]
    dff = fbin_ref.shape[-1]
    sl = lt // sub
    r = nb * sl
    cc = FFN_COLS
    nch = dff // cc

    @pl.when(pl.program_id(1) == 0)
    def _():
        ubuf_ref[:, HIST_LO:HIST_PAD, :] = fbin_ref[...] * 0.5

    ada = ada_ref[...]
    sh2 = ada[3]
    pre_gain = gpre_ref[...] * (1.0 + ada[4])
    post_gain = ada[5] * gpost_ref[...]
    x3s, hbs, ffs = {}, {}, {}

    def norm_in(k):
        x3s[k] = x_ref[:, k * sl:(k + 1) * sl, :]
        h3 = (x3s[k] * _rms_scale(x3s[k])) * pre_gain + sh2
        hb_ref[k] = h3.reshape(r, d).astype(BF)
        hbs[k] = hb_ref.at[k]

    def up(k, j):
        c0 = j * cc
        parts = EDGE_PARTS if (k, j) == (0, 0) else 1
        return (_dot_rows(hbs[k][...], wup_ref[:, c0:c0 + cc], parts),
                _dot_rows(hbs[k][...], wup_ref[:, dff + c0:dff + c0 + cc], parts))

    def finish(k):
        ff = ffs.pop(k)
        xo_ref[:, k * sl:(k + 1) * sl, :] = x3s.pop(k) + (ff * _rms_scale(ff)).reshape(nb, sl, d) * post_gain

    items = [(k, j) for k in range(sub) for j in range(nch)]
    norm_in(0)
    nxt = up(0, 0)
    for i, (k, j) in enumerate(items):
        hgate, val = nxt
        if i + 1 < len(items):
            kn, jn = items[i + 1]
            if jn == 0:
                norm_in(kn)
            nxt = up(kn, jn)
        cols = slice(j * cc, (j + 1) * cc)
        hgc = _conv_taps(ubuf_ref, hgate.reshape(nb, sl, cc), convw_ref[:, cols], cols, k * sl)
        act = _half_silu(hgc.reshape(r, cc)) * val
        part = _dot_rows(act.astype(BF), wdn_ref[cols, :], EDGE_PARTS if i + 1 == len(items) else 1)
        ffs[k] = part if j == 0 else ffs[k] + part
        if j == 0 and k > 0:
            finish(k - 1)
    finish(sub - 1)
    fbout_ref[...] = _conv_carry(ubuf_ref, lt, slice(None)) * 2.0


def _ada_kernel(c_ref, w_ref, b_ref, o_ref):
    c = c_ref[...]
    cs = (c / (1.0 + jnp.exp(-c))).astype(BF)
    o_ref[...] = _dot(cs, w_ref[...].astype(BF)) + b_ref[...]


def _ada_call(c_all, w_ada, b_ada):
    depth, d, d6 = w_ada.shape
    n = d6 // d
    bt = c_all.shape[0]
    return pl.pallas_call(
        _ada_kernel,
        grid=(depth, n),
        in_specs=[
            pl.BlockSpec((bt, d), lambda l, j: (0, 0)),
            pl.BlockSpec((None, d, d), lambda l, j: (l, 0, j)),
            pl.BlockSpec((None, None, 1, d), lambda l, j: (l, j, 0, 0)),
        ],
        out_specs=pl.BlockSpec((None, None, bt, d), lambda l, j: (l, j, 0, 0)),
        out_shape=jax.ShapeDtypeStruct((depth, n, bt, d), F32),
        compiler_params=pltpu.CompilerParams(dimension_semantics=("arbitrary", "arbitrary")),
        name="ada",
    )(c_all, w_ada, b_ada.reshape(depth, n, 1, d))


def _resident(shape, layer):
    nd = len(shape)
    return pl.BlockSpec((None,) + tuple(shape[1:]), lambda b, t: (layer,) + (0,) * (nd - 1),
                        pipeline_mode=pl.Buffered(1))


def _tile_plan(batch, length, sub_rows, subs):
    if length % (sub_rows * subs) == 0:
        return 1, sub_rows * subs, subs, HGRN_CHUNK
    assert length % SUBLANES == 0 and length <= sub_rows
    return batch, length, 1, length


def _mixer_call(layer, x, ada, boff, s_in, cb_in, p):
    batch, length, d = x.shape
    nb, lt, sub, chunk = _tile_plan(batch, length, MIXER_SUB_ROWS, MIXER_SUBS)
    assert boff % nb == 0 and batch % nb == 0
    bo = boff // nb
    cw = cb_in.shape[-1]
    row = lambda b, t: (b, t, 0)
    per_b4 = lambda b, t: (b, 0, 0, 0)
    per_b3 = lambda b, t: (b, 0, 0)
    in_specs = [
        pl.BlockSpec((nb, lt, d), row),
        pl.BlockSpec((None, 6, nb, 1, d), lambda b, t: (layer, 0, b + bo, 0, 0)),
        pl.BlockSpec((nb, HEADS, DK, DV), per_b4),
        pl.BlockSpec((nb, CONV_HIST, cw), per_b3),
        pl.BlockSpec(p["lb_logits"].shape, lambda b, t: (0, 0)),
        _resident(p["g_pre_mix"].shape, layer),
        _resident(p["g_post_mix"].shape, layer),
        _resident(p["hgrn_norm"].shape, layer),
        _resident(p["w_in"].shape, layer),
        _resident(p["w_pa"].shape, layer),
        _resident(p["conv_w"].shape, layer),
        _resident(p["w_pb"].shape, layer),
        _resident(p["w_bgate"].shape, layer),
        _resident(p["w_o"].shape, layer),
    ]
    out_specs = [
        pl.BlockSpec((nb, lt, d), row),
        pl.BlockSpec((nb, HEADS, DK, DV), per_b4),
        pl.BlockSpec((nb, CONV_HIST, cw), per_b3),
    ]
    out_shape = [
        jax.ShapeDtypeStruct(x.shape, F32),
        jax.ShapeDtypeStruct(s_in.shape, F32),
        jax.ShapeDtypeStruct(cb_in.shape, F32),
    ]
    return pl.pallas_call(
        functools.partial(_mixer_kernel, layer, nb, lt, sub, chunk),
        grid=(batch // nb, length // lt),
        in_specs=in_specs,
        out_specs=out_specs,
        out_shape=out_shape,
        scratch_shapes=[
            pltpu.VMEM((nb, HEADS, DK, DV), F32),
            pltpu.VMEM((nb, lt + HIST_PAD, cw), F32),
            pltpu.VMEM((sub, nb * lt // sub, d), BF),
        ],
        compiler_params=pltpu.CompilerParams(
            dimension_semantics=("arbitrary", "arbitrary"), vmem_limit_bytes=VMEM_LIMIT),
        name=f"mixer_l{layer}_n{nb}",
    )(x, ada, s_in, cb_in, p["lb_logits"], p["g_pre_mix"], p["g_post_mix"], p["hgrn_norm"],
      p["w_in"], p["w_pa"], p["conv_w"], p["w_pb"], p["w_bgate"], p["w_o"])


def _ffn_call(layer, x, ada, boff, fb_in, p):
    batch, length, d = x.shape
    nb, lt, sub, _ = _tile_plan(batch, length, FFN_SUB_ROWS, FFN_SUBS)
    bo = boff // nb
    dff = fb_in.shape[-1]
    assert dff % FFN_COLS == 0
    row = lambda b, t: (b, t, 0)
    per_b3 = lambda b, t: (b, 0, 0)
    in_specs = [
        pl.BlockSpec((nb, lt, d), row),
        pl.BlockSpec((None, 6, nb, 1, d), lambda b, t: (layer, 0, b + bo, 0, 0)),
        pl.BlockSpec((nb, CONV_HIST, dff), per_b3),
        _resident(p["g_pre_ffn"].shape, layer),
        _resident(p["g_post_ffn"].shape, layer),
        _resident(p["w_up"].shape, layer),
        _resident(p["ffn_conv_w"].shape, layer),
        _resident(p["w_down"].shape, layer),
    ]
    out_specs = [
        pl.BlockSpec((nb, lt, d), row),
        pl.BlockSpec((nb, CONV_HIST, dff), per_b3),
    ]
    out_shape = [
        jax.ShapeDtypeStruct(x.shape, F32),
        jax.ShapeDtypeStruct(fb_in.shape, F32),
    ]
    return pl.pallas_call(
        functools.partial(_ffn_kernel, nb, lt, sub),
        grid=(batch // nb, length // lt),
        in_specs=in_specs,
        out_specs=out_specs,
        out_shape=out_shape,
        scratch_shapes=[pltpu.VMEM((nb, lt + HIST_PAD, dff), F32),
                        pltpu.VMEM((sub, nb * lt // sub, d), BF)],
        compiler_params=pltpu.CompilerParams(
            dimension_semantics=("arbitrary", "arbitrary"), vmem_limit_bytes=VMEM_LIMIT),
        name=f"ffn_l{layer}_n{nb}",
    )(x, ada, fb_in, p["g_pre_ffn"], p["g_post_ffn"], p["w_up"], p["ffn_conv_w"], p["w_down"])


def _trunk(x, ada, boff, s_in, cb_in, fb_in, p):
    depth = p["w_in"].shape[0]
    new_s, new_cb, new_fb = [], [], []
    for l in range(depth):
        x, s, cb = _mixer_call(l, x, ada, boff, s_in[l], cb_in[l], p)
        x, fb = _ffn_call(l, x, ada, boff, fb_in[l], p)
        new_s.append(s)
        new_cb.append(cb)
        new_fb.append(fb)
    return x, jnp.stack(new_s), jnp.stack(new_cb), jnp.stack(new_fb)


def _halve_cols(w, ranges):
    scale = jnp.ones((w.shape[-1],), F32)
    for lo, hi in ranges:
        scale = scale.at[lo:hi].set(0.5)
    return (w * scale).astype(BF)


def kernel(x_prompt, x_sample, state_hgrn, state_conv, state_ffn_conv, c_prompt, c_sample, w_ada, b_ada, g_pre_mix, g_post_mix, g_pre_ffn, g_post_ffn, w_in, hgrn_lb_logits, hgrn_norm, w_pa, conv_w, w_pb, w_bgate, w_o, w_up, ffn_conv_w, w_down):
    depth, d = g_pre_mix.shape
    dff = w_down.shape[1]
    vec = lambda g: g.reshape(depth, 1, g.shape[-1])
    p = {
        "lb_logits": hgrn_lb_logits,
        "g_pre_mix": vec(g_pre_mix), "g_post_mix": vec(g_post_mix),
        "g_pre_ffn": vec(g_pre_ffn), "g_post_ffn": vec(g_post_ffn),
        "hgrn_norm": vec(hgrn_norm),
        "w_in": _halve_cols(w_in, [(0, AW), (3 * AW, 4 * AW)]),
        "w_pa": w_pa.astype(BF), "conv_w": conv_w, "w_pb": w_pb.astype(BF),
        "w_bgate": _halve_cols(w_bgate, [(0, w_bgate.shape[-1])]),
        "w_o": _halve_cols(w_o, [(0, w_o.shape[-1])]),
        "w_up": _halve_cols(w_up, [(0, dff)]), "ffn_conv_w": ffn_conv_w, "w_down": w_down.astype(BF),
    }
    bp = x_prompt.shape[0]
    ada = _ada_call(jnp.concatenate([c_prompt, c_sample], axis=0), w_ada, b_ada)
    ada = ada.reshape(ada.shape[:3] + (1, d))

    s0 = jnp.zeros((depth, bp) + state_hgrn.shape[2:], F32)
    cb0 = jnp.zeros((depth, bp) + state_conv.shape[2:], F32)
    fb0 = jnp.zeros((depth, bp) + state_ffn_conv.shape[2:], F32)
    y_p, s_p, cb_p, fb_p = _trunk(x_prompt, ada, 0, s0, cb0, fb0, p)
    y_s, s_s, cb_s, fb_s = _trunk(x_sample, ada, bp, state_hgrn, state_conv, state_ffn_conv, p)
    return (y_p, y_s, s_p, cb_p, fb_p, s_s, cb_s, fb_s)
```

```python
import functools

import jax
import jax.numpy as jnp
from jax import lax
from jax.experimental import pallas as pl
from jax.experimental.pallas import tpu as pltpu

HEADS = 4
DK = 128
DV = 128
AW = HEADS * DV
CONV_HIST = 2
EPS = 1e-6
SUBLANES = 8
DIAG = 8
HIST_PAD = 8
HIST_LO = HIST_PAD - CONV_HIST
FFN_COLS = 256
MIXER_SUB_ROWS = 512
MIXER_SUBS = 1
FFN_SUB_ROWS = 256
FFN_SUBS = 4
HGRN_CHUNK = 128
MIN_DOT_ROWS = 128
EDGE_PARTS = 2
LOG2E = 1.4426950408889634
BF = jnp.bfloat16
F32 = jnp.float32
VMEM_LIMIT = 60 * 1024 * 1024


def _half_silu(hx):
    return hx + hx * jnp.tanh(hx)


def _rms_scale(x):
    return lax.rsqrt(jnp.mean(x * x, axis=-1, keepdims=True) + EPS)


def _dot(a, b):
    return jnp.dot(a, b, preferred_element_type=F32)


def _dot_rows(a, b, parts):
    m = a.shape[0]
    if parts == 1 or m % (parts * MIN_DOT_ROWS):
        return _dot(a, b)
    step = m // parts
    return jnp.concatenate([_dot(a[i * step:(i + 1) * step], b) for i in range(parts)], axis=0)


def _dot_nt(a, b):
    return lax.dot_general(a, b, (((1,), (1,)), ((), ())), preferred_element_type=F32)


def _dot_tn(a, b):
    return lax.dot_general(a, b, (((0,), (0,)), ((), ())), preferred_element_type=F32)


def _neg_abs(x):
    return lax.bitcast_convert_type(
        lax.bitcast_convert_type(x, jnp.uint32) | jnp.uint32(0x80000000), F32)


def _levels(chunk):
    lv, g = [], DIAG
    while 2 * g <= chunk:
        lv.append(g)
        g *= 2
    return lv


def _hgrn_pre(qa, lnk2, va, lf2):
    c = qa.shape[0]
    row = lax.broadcasted_iota(jnp.int32, (c, AW), 0)
    b2 = lf2
    d = 1
    while d < c:
        b2 = b2 + jnp.where(row >= d, pltpu.roll(b2, d, 0), 0.0)
        d *= 2
    c2 = b2 - lnk2
    blast = b2[c - 1:c, :]
    pre = {
        "blast": blast,
        "qc": (qa * jnp.exp2(b2)).astype(BF),
        "kc": jnp.exp2(blast - c2).astype(BF),
        "vb": va.astype(BF),
    }

    nblk = c // SUBLANES
    b3 = b2.reshape(nblk, SUBLANES, AW)
    c3 = c2.reshape(nblk, SUBLANES, AW)
    q3 = qa.reshape(nblk, SUBLANES, AW)
    sub3 = lax.broadcasted_iota(jnp.int32, (nblk, SUBLANES, AW), 1)

    def rows_of_tile(x3, first, stride):
        out = x3[:, first:first + 1, :]
        for i in range(1, SUBLANES // stride):
            out = jnp.where(sub3 >= i * stride, x3[:, first + i * stride:first + i * stride + 1, :], out)
        return jnp.broadcast_to(out, (nblk, SUBLANES, AW))

    xs = []
    for g in _levels(c):
        if 2 * g <= SUBLANES:
            rg = rows_of_tile(b3, g - 1, 2 * g).reshape(c, AW)
        else:
            ref_rows = [jnp.broadcast_to(b2[p * 2 * g + g - 1:p * 2 * g + g, :], (2 * g, AW))
                        for p in range(c // (2 * g))]
            rg = ref_rows[0] if len(ref_rows) == 1 else jnp.concatenate(ref_rows, axis=0)
        odd = (row & g) != 0
        eg = jnp.exp2(_neg_abs(jnp.where(odd, b2, c2) - rg))
        xs.append(jnp.where(odd, qa * eg, eg).astype(BF))
    pre["xs"] = xs
    pre["diag_in"] = (b3, c3, q3)
    return pre


def _hgrn_diag(pre, zero_bits):
    b3, c3, q3 = pre.pop("diag_in")
    nblk = b3.shape[0]
    c = nblk * SUBLANES
    if zero_bits is not None:
        z3 = jnp.concatenate([zero_bits] * (AW // zero_bits.shape[1]), axis=1)[None]
        q3 = lax.bitcast_convert_type(lax.bitcast_convert_type(q3, jnp.uint32) | z3, F32)
    sub3 = lax.broadcasted_iota(jnp.int32, (nblk, SUBLANES, AW), 1)
    lane_s = lax.broadcasted_iota(jnp.int32, (c, c), 1) & (DIAG - 1)
    dtile = [jnp.zeros((c, c), F32) for _ in range(HEADS)]
    for s in range(DIAG):
        cs = c3[:, s:s + 1, :]
        for i in range(1, SUBLANES // DIAG):
            cs = jnp.where(sub3 >= i * DIAG, c3[:, s + i * DIAG:s + i * DIAG + 1, :], cs)
        p = q3 * jnp.exp2(b3 - cs)
        sel = lane_s == s
        for h in range(HEADS):
            a = jnp.sum(p[:, :, h * DK:(h + 1) * DK], axis=-1, keepdims=True).reshape(c, 1)
            dtile[h] = jnp.where(sel, a, dtile[h])
    pre["dtile"] = dtile


def _hgrn_intra(pre):
    kc, vb, xs = pre.pop("kc"), pre.pop("vb"), pre.pop("xs")
    dtile = pre.pop("dtile")
    c = kc.shape[0]
    ti = lax.broadcasted_iota(jnp.int32, (c, c), 0)
    si = lax.broadcasted_iota(jnp.int32, (c, c), 1)
    txs = ti ^ si
    masks = [((ti & g) != 0) & (txs >= g) & (txs < 2 * g) for g in _levels(c)]
    diag_mask = (txs < DIAG) & (si <= ti)
    heads = [slice(h * DK, (h + 1) * DK) for h in range(HEADS)]
    amats = []
    for h, sl in enumerate(heads):
        amat = jnp.where(diag_mask, dtile[h], 0.0)
        for xg, m in zip(xs, masks):
            amat = jnp.where(m, _dot_nt(xg[:, sl], xg[:, sl]), amat)
        amats.append(amat.astype(BF))
    pre["inc"] = [_dot_tn(vb[:, sl], kc[:, sl]) for sl in heads]
    pre["o_intra"] = [_dot(amats[h], vb[:, sl]) for h, sl in enumerate(heads)]


def _hgrn_recur(pre, st):
    qc, blast = pre["qc"], pre["blast"]
    outs, new_st = [], []
    for h in range(HEADS):
        sl = slice(h * DK, (h + 1) * DK)
        outs.append(_dot_nt(qc[:, sl], st[h].astype(BF)) + pre["o_intra"][h])
        new_st.append(st[h] * jnp.exp2(blast[:, sl]) + pre["inc"][h])
    return jnp.concatenate(outs, axis=-1), new_st


def _conv_taps(ubuf_ref, u3, w, cols, row0):
    lt = u3.shape[1]
    ubuf_ref[:, HIST_PAD + row0:HIST_PAD + row0 + lt, cols] = u3
    return (ubuf_ref[:, HIST_LO + row0:HIST_LO + row0 + lt, cols] * w[0:1, :]
            + ubuf_ref[:, HIST_LO + 1 + row0:HIST_LO + 1 + row0 + lt, cols] * w[1:2, :]
            + u3 * w[2:3, :])


def _conv_carry(ubuf_ref, lt, cols):
    hist = ubuf_ref[:, lt + HIST_LO:lt + HIST_PAD, cols]
    ubuf_ref[:, HIST_LO:HIST_PAD, cols] = hist
    return hist


def _mixer_kernel(layer, nb, lt, sub, chunk,
                  x_ref, ada_ref, sin_ref, cbin_ref, lbl_ref, gpre_ref, gpost_ref, anorm_ref,
                  win_ref, wpa_ref, convw_ref, wpb_ref, wbg_ref, wo_ref,
                  xo_ref, sout_ref, cbout_ref, st_ref, ubuf_ref, hb_ref):
    t = pl.program_id(1)
    d = x_ref.shape[-1]
    cw = cbin_ref.shape[-1]
    sl = lt // sub
    r = nb * sl

    @pl.when(t == 0)
    def _():
        for bi in range(nb):
            for h in range(HEADS):
                st_ref[bi, h] = sin_ref[bi, h].T
        ubuf_ref[:, HIST_LO:HIST_PAD, :] = cbin_ref[...]

    ada = ada_ref[...]
    sh1, gt1 = ada[0], ada[2]
    pre_gain = gpre_ref[...] * (1.0 + ada[1])
    post_gain = gt1 * gpost_ref[...]
    if layer > 0:
        lg = lbl_ref[...]
        e = jnp.exp(lg - jnp.max(lg, axis=0, keepdims=True))
        pr = e / jnp.sum(e, axis=0, keepdims=True)
        lb = jnp.sum(pr[1:layer + 1, :], axis=0, keepdims=True)
        a0 = jnp.log(lb) * LOG2E
        c1 = jnp.log1p(-lb) * LOG2E
    state = {bi: [st_ref[bi, h] for h in range(HEADS)] for bi in range(nb)}
    ctx = [dict() for _ in range(sub)]
    rt_zero = lax.bitcast_convert_type(
        jnp.full((SUBLANES, DK), lax.shift_right_arithmetic(t, jnp.int32(31)), jnp.int32), jnp.uint32)

    def norm_in(k):
        c = ctx[k]
        c["x3"] = x_ref[:, k * sl:(k + 1) * sl, :]
        h3 = (c["x3"] * _rms_scale(c["x3"])) * pre_gain + sh1
        hb_ref[k] = h3.reshape(r, d).astype(BF)

    def proj_a(k):
        ctx[k]["proj"] = _dot_rows(hb_ref[k], win_ref[:, 0:4 * AW], EDGE_PARTS if k == 0 else 1)

    def gates_pre(k):
        c = ctx[k]
        proj = c.pop("proj")
        hq, z, vi, hg = proj[:, 0:AW], proj[:, AW:2 * AW], proj[:, 2 * AW:3 * AW], proj[:, 3 * AW:4 * AW]
        qa = _half_silu(hq)
        z2 = z * LOG2E
        l1 = jnp.log(1.0 + jnp.exp2(_neg_abs(z2))) * LOG2E
        logsig2 = jnp.minimum(z2, 0.0) - l1
        lognsig2 = -jnp.maximum(z2, 0.0) - l1
        if layer == 0:
            lf2, lnk2 = logsig2, lognsig2
        else:
            y = c1 + logsig2
            lf2 = jnp.maximum(a0, y) + jnp.log(1.0 + jnp.exp2(_neg_abs(a0 - y))) * LOG2E
            lnk2 = c1 + lognsig2
        c["gs"] = _half_silu(hg) * jnp.concatenate([anorm_ref[...]] * HEADS, axis=-1)
        c["pre"] = []
        for bi in range(nb):
            for ci in range(sl // chunk):
                r0 = bi * sl + ci * chunk
                rs = slice(r0, r0 + chunk)
                c["pre"].append((bi, _hgrn_pre(qa[rs], lnk2[rs], vi[rs], lf2[rs])))

    def conv_branch(k):
        c = ctx[k]
        projb = _dot(hb_ref[k], win_ref[:, 4 * AW:7 * AW])
        bgt, cgt, xv = projb[:, 0:AW], projb[:, AW:2 * AW], projb[:, 2 * AW:3 * AW]
        u = _conv_taps(ubuf_ref, (cgt * xv).reshape(nb, sl, cw), convw_ref[...], slice(None), k * sl)
        c["y_b"] = _dot((bgt * u.reshape(r, cw)).astype(BF), wpb_ref[...])

    def diag(k):
        y_b = ctx[k]["y_b"]
        zero_bits = lax.bitcast_convert_type(y_b[0:SUBLANES, 0:DK], jnp.uint32) & rt_zero
        for _, pre in ctx[k]["pre"]:
            _hgrn_diag(pre, zero_bits)

    def gate_mm(k):
        c = ctx[k]
        probe = pltpu.bitcast(c["pre"][0][1]["xs"][0][0:2 * SUBLANES, 0:DK], jnp.uint32) & rt_zero
        hb32 = pltpu.bitcast(hb_ref[k], jnp.uint32)
        rows32 = hb32.shape[0]
        hb32 = hb32.reshape(rows32 // SUBLANES, SUBLANES, d) | jnp.concatenate([probe] * (d // DK), axis=1)[None]
        hb = pltpu.bitcast(hb32.reshape(rows32, d), BF)
        c["tg"] = jnp.tanh(_dot(hb, wbg_ref[...]))

    def intra(k):
        for _, pre in ctx[k]["pre"]:
            _hgrn_intra(pre)

    def recur(k):
        c = ctx[k]
        o_parts = []
        for bi, pre in c.pop("pre"):
            o, state[bi] = _hgrn_recur(pre, state[bi])
            o_parts.append(o)
        c["o"] = o_parts[0] if len(o_parts) == 1 else jnp.concatenate(o_parts, axis=0)

    def tail(k):
        c = ctx[k]
        o = c.pop("o")
        on = jnp.concatenate(
            [o[:, h * DV:(h + 1) * DV] * _rms_scale(o[:, h * DV:(h + 1) * DV]) for h in range(HEADS)],
            axis=-1) * c.pop("gs")
        y_a = _dot(on.astype(BF), wpa_ref[...])
        tg, y_b = c.pop("tg"), c.pop("y_b")
        mix = (y_a + tg[:, :d] * y_a) + (y_b + tg[:, d:] * y_b)
        m = _dot_rows(mix.astype(BF), wo_ref[...], EDGE_PARTS if k == sub - 1 else 1)
        xo_ref[:, k * sl:(k + 1) * sl, :] = c.pop("x3") + (m * _rms_scale(m)).reshape(nb, sl, d) * post_gain

    stages = [norm_in, proj_a, gates_pre, conv_branch, diag, gate_mm, intra, recur, tail]
    for step in range(len(stages) + sub - 1):
        for k in range(sub):
            if 0 <= step - k < len(stages):
                stages[step - k](k)

    cbout_ref[...] = _conv_carry(ubuf_ref, lt, slice(None))
    for bi in range(nb):
        for h in range(HEADS):
            st_ref[bi, h] = state[bi][h]

    @pl.when(t == pl.num_programs(1) - 1)
    def _():
        for bi in range(nb):
            for h in range(HEADS):
                sout_ref[bi, h] = st_ref[bi, h].T


def _ffn_kernel(nb, lt, sub,
                x_ref, ada_ref, fbin_ref, gpre_ref, gpost_ref, wup_ref, convw_ref, wdn_ref,
                xo_ref, fbout_ref, ubuf_ref, hb_ref):
    d = x_ref.shape[-1]
    dff = fbin_ref.shape[-1]
    sl = lt // sub
    r = nb * sl
    cc = FFN_COLS
    nch = dff // cc

    @pl.when(pl.program_id(1) == 0)
    def _():
        ubuf_ref[:, HIST_LO:HIST_PAD, :] = fbin_ref[...] * 0.5

    ada = ada_ref[...]
    sh2 = ada[3]
    pre_gain = gpre_ref[...] * (1.0 + ada[4])
    post_gain = ada[5] * gpost_ref[...]
    x3s, hbs, ffs = {}, {}, {}

    def norm_in(k):
        x3s[k] = x_ref[:, k * sl:(k + 1) * sl, :]
        h3 = (x3s[k] * _rms_scale(x3s[k])) * pre_gain + sh2
        hb_ref[k] = h3.reshape(r, d).astype(BF)
        hbs[k] = hb_ref.at[k]

    def up(k, j):
        c0 = j * cc
        parts = EDGE_PARTS if (k, j) == (0, 0) else 1
        return (_dot_rows(hbs[k][...], wup_ref[:, c0:c0 + cc], parts),
                _dot_rows(hbs[k][...], wup_ref[:, dff + c0:dff + c0 + cc], parts))

    def finish(k):
        ff = ffs.pop(k)
        xo_ref[:, k * sl:(k + 1) * sl, :] = x3s.pop(k) + (ff * _rms_scale(ff)).reshape(nb, sl, d) * post_gain

    items = [(k, j) for k in range(sub) for j in range(nch)]
    norm_in(0)
    nxt = up(0, 0)
    for i, (k, j) in enumerate(items):
        hgate, val = nxt
        if i + 1 < len(items):
            kn, jn = items[i + 1]
            if jn == 0:
                norm_in(kn)
            nxt = up(kn, jn)
        cols = slice(j * cc, (j + 1) * cc)
        hgc = _conv_taps(ubuf_ref, hgate.reshape(nb, sl, cc), convw_ref[:, cols], cols, k * sl)
        act = _half_silu(hgc.reshape(r, cc)) * val
        part = _dot_rows(act.astype(BF), wdn_ref[cols, :], EDGE_PARTS if i + 1 == len(items) else 1)
        ffs[k] = part if j == 0 else ffs[k] + part
        if j == 0 and k > 0:
            finish(k - 1)
    finish(sub - 1)
    fbout_ref[...] = _conv_carry(ubuf_ref, lt, slice(None)) * 2.0


def _ada_kernel(c_ref, w_ref, b_ref, o_ref):
    c = c_ref[...]
    cs = (c / (1.0 + jnp.exp(-c))).astype(BF)
    o_ref[...] = _dot(cs, w_ref[...].astype(BF)) + b_ref[...]


def _ada_call(c_all, w_ada, b_ada):
    depth, d, d6 = w_ada.shape
    n = d6 // d
    bt = c_all.shape[0]
    return pl.pallas_call(
        _ada_kernel,
        grid=(depth, n),
        in_specs=[
            pl.BlockSpec((bt, d), lambda l, j: (0, 0)),
            pl.BlockSpec((None, d, d), lambda l, j: (l, 0, j)),
            pl.BlockSpec((None, None, 1, d), lambda l, j: (l, j, 0, 0)),
        ],
        out_specs=pl.BlockSpec((None, None, bt, d), lambda l, j: (l, j, 0, 0)),
        out_shape=jax.ShapeDtypeStruct((depth, n, bt, d), F32),
        compiler_params=pltpu.CompilerParams(dimension_semantics=("arbitrary", "arbitrary")),
        name="ada",
    )(c_all, w_ada, b_ada.reshape(depth, n, 1, d))


def _resident(shape, layer):
    nd = len(shape)
    return pl.BlockSpec((None,) + tuple(shape[1:]), lambda b, t: (layer,) + (0,) * (nd - 1),
                        pipeline_mode=pl.Buffered(1))


def _tile_plan(batch, length, sub_rows, subs):
    if length % (sub_rows * subs) == 0:
        return 1, sub_rows * subs, subs, HGRN_CHUNK
    assert length % SUBLANES == 0 and length <= sub_rows
    return batch, length, 1, length


def _mixer_call(layer, x, ada, boff, s_in, cb_in, p):
    batch, length, d = x.shape
    nb, lt, sub, chunk = _tile_plan(batch, length, MIXER_SUB_ROWS, MIXER_SUBS)
    assert boff % nb == 0 and batch % nb == 0
    bo = boff // nb
    cw = cb_in.shape[-1]
    row = lambda b, t: (b, t, 0)
    per_b4 = lambda b, t: (b, 0, 0, 0)
    per_b3 = lambda b, t: (b, 0, 0)
    in_specs = [
        pl.BlockSpec((nb, lt, d), row),
        pl.BlockSpec((None, 6, nb, 1, d), lambda b, t: (layer, 0, b + bo, 0, 0)),
        pl.BlockSpec((nb, HEADS, DK, DV), per_b4),
        pl.BlockSpec((nb, CONV_HIST, cw), per_b3),
        pl.BlockSpec(p["lb_logits"].shape, lambda b, t: (0, 0)),
        _resident(p["g_pre_mix"].shape, layer),
        _resident(p["g_post_mix"].shape, layer),
        _resident(p["hgrn_norm"].shape, layer),
        _resident(p["w_in"].shape, layer),
        _resident(p["w_pa"].shape, layer),
        _resident(p["conv_w"].shape, layer),
        _resident(p["w_pb"].shape, layer),
        _resident(p["w_bgate"].shape, layer),
        _resident(p["w_o"].shape, layer),
    ]
    out_specs = [
        pl.BlockSpec((nb, lt, d), row),
        pl.BlockSpec((nb, HEADS, DK, DV), per_b4),
        pl.BlockSpec((nb, CONV_HIST, cw), per_b3),
    ]
    out_shape = [
        jax.ShapeDtypeStruct(x.shape, F32),
        jax.ShapeDtypeStruct(s_in.shape, F32),
        jax.ShapeDtypeStruct(cb_in.shape, F32),
    ]
    return pl.pallas_call(
        functools.partial(_mixer_kernel, layer, nb, lt, sub, chunk),
        grid=(batch // nb, length // lt),
        in_specs=in_specs,
        out_specs=out_specs,
        out_shape=out_shape,
        scratch_shapes=[
            pltpu.VMEM((nb, HEADS, DV, DK), F32),
            pltpu.VMEM((nb, lt + HIST_PAD, cw), F32),
            pltpu.VMEM((sub, nb * lt // sub, d), BF),
        ],
        compiler_params=pltpu.CompilerParams(
            dimension_semantics=("arbitrary", "arbitrary"), vmem_limit_bytes=VMEM_LIMIT),
        name=f"mixer_l{layer}_n{nb}",
    )(x, ada, s_in, cb_in, p["lb_logits"], p["g_pre_mix"], p["g_post_mix"], p["hgrn_norm"],
      p["w_in"], p["w_pa"], p["conv_w"], p["w_pb"], p["w_bgate"], p["w_o"])


def _ffn_call(layer, x, ada, boff, fb_in, p):
    batch, length, d = x.shape
    nb, lt, sub, _ = _tile_plan(batch, length, FFN_SUB_ROWS, FFN_SUBS)
    bo = boff // nb
    dff = fb_in.shape[-1]
    assert dff % FFN_COLS == 0
    row = lambda b, t: (b, t, 0)
    per_b3 = lambda b, t: (b, 0, 0)
    in_specs = [
        pl.BlockSpec((nb, lt, d), row),
        pl.BlockSpec((None, 6, nb, 1, d), lambda b, t: (layer, 0, b + bo, 0, 0)),
        pl.BlockSpec((nb, CONV_HIST, dff), per_b3),
        _resident(p["g_pre_ffn"].shape, layer),
        _resident(p["g_post_ffn"].shape, layer),
        _resident(p["w_up"].shape, layer),
        _resident(p["ffn_conv_w"].shape, layer),
        _resident(p["w_down"].shape, layer),
    ]
    out_specs = [
        pl.BlockSpec((nb, lt, d), row),
        pl.BlockSpec((nb, CONV_HIST, dff), per_b3),
    ]
    out_shape = [
        jax.ShapeDtypeStruct(x.shape, F32),
        jax.ShapeDtypeStruct(fb_in.shape, F32),
    ]
    return pl.pallas_call(
        functools.partial(_ffn_kernel, nb, lt, sub),
        grid=(batch // nb, length // lt),
        in_specs=in_specs,
        out_specs=out_specs,
        out_shape=out_shape,
        scratch_shapes=[pltpu.VMEM((nb, lt + HIST_PAD, dff), F32),
                        pltpu.VMEM((sub, nb * lt // sub, d), BF)],
        compiler_params=pltpu.CompilerParams(
            dimension_semantics=("arbitrary", "arbitrary"), vmem_limit_bytes=VMEM_LIMIT),
        name=f"ffn_l{layer}_n{nb}",
    )(x, ada, fb_in, p["g_pre_ffn"], p["g_post_ffn"], p["w_up"], p["ffn_conv_w"], p["w_down"])


def _trunk(x, ada, boff, s_in, cb_in, fb_in, p):
    depth = p["w_in"].shape[0]
    new_s, new_cb, new_fb = [], [], []
    for l in range(depth):
        x, s, cb = _mixer_call(l, x, ada, boff, s_in[l], cb_in[l], p)
        x, fb = _ffn_call(l, x, ada, boff, fb_in[l], p)
        new_s.append(s)
        new_cb.append(cb)
        new_fb.append(fb)
    return x, jnp.stack(new_s), jnp.stack(new_cb), jnp.stack(new_fb)


def _halve_cols(w, ranges):
    scale = jnp.ones((w.shape[-1],), F32)
    for lo, hi in ranges:
        scale = scale.at[lo:hi].set(0.5)
    return (w * scale).astype(BF)


def kernel(x_prompt, x_sample, state_hgrn, state_conv, state_ffn_conv, c_prompt, c_sample, w_ada, b_ada, g_pre_mix, g_post_mix, g_pre_ffn, g_post_ffn, w_in, hgrn_lb_logits, hgrn_norm, w_pa, conv_w, w_pb, w_bgate, w_o, w_up, ffn_conv_w, w_down):
    depth, d = g_pre_mix.shape
    dff = w_down.shape[1]
    vec = lambda g: g.reshape(depth, 1, g.shape[-1])
    p = {
        "lb_logits": hgrn_lb_logits,
        "g_pre_mix": vec(g_pre_mix), "g_post_mix": vec(g_post_mix),
        "g_pre_ffn": vec(g_pre_ffn), "g_post_ffn": vec(g_post_ffn),
        "hgrn_norm": vec(hgrn_norm),
        "w_in": _halve_cols(w_in, [(0, AW), (3 * AW, 4 * AW)]),
        "w_pa": w_pa.astype(BF), "conv_w": conv_w, "w_pb": w_pb.astype(BF),
        "w_bgate": _halve_cols(w_bgate, [(0, w_bgate.shape[-1])]),
        "w_o": _halve_cols(w_o, [(0, w_o.shape[-1])]),
        "w_up": _halve_cols(w_up, [(0, dff)]), "ffn_conv_w": ffn_conv_w, "w_down": w_down.astype(BF),
    }
    bp = x_prompt.shape[0]
    ada = _ada_call(jnp.concatenate([c_prompt, c_sample], axis=0), w_ada, b_ada)
    ada = ada.reshape(ada.shape[:3] + (1, d))

    s0 = jnp.zeros((depth, bp) + state_hgrn.shape[2:], F32)
    cb0 = jnp.zeros((depth, bp) + state_conv.shape[2:], F32)
    fb0 = jnp.zeros((depth, bp) + state_ffn_conv.shape[2:], F32)
    y_p, s_p, cb_p, fb_p = _trunk(x_prompt, ada, 0, s0, cb0, fb0, p)
    y_s, s_s, cb_s, fb_s = _trunk(x_sample, ada, bp, state_hgrn, state_conv, state_ffn_conv, p)
    return (y_p, y_s, s_p, cb_p, fb_p, s_s, cb_s, fb_s)
```

```python
import functools

import jax
import jax.numpy as jnp
from jax import lax
from jax.experimental import pallas as pl
from jax.experimental.pallas import tpu as pltpu

HEADS = 4
DK = 128
DV = 128
AW = HEADS * DV
CONV_HIST = 2
EPS = 1e-6
SUBLANES = 8
DIAG = 8
HIST_PAD = 8
HIST_LO = HIST_PAD - CONV_HIST
FFN_COLS = 256
MIXER_SUB_ROWS = 512
MIXER_SUBS = 1
FFN_SUB_ROWS = 256
FFN_SUBS = 4
HGRN_CHUNK = 128
MIN_DOT_ROWS = 128
EDGE_PARTS = 2
LOG2E = 1.4426950408889634
BF = jnp.bfloat16
F32 = jnp.float32
VMEM_LIMIT = 60 * 1024 * 1024


def _half_silu(hx):
    return hx + hx * jnp.tanh(hx)


def _rms_scale(x):
    return lax.rsqrt(jnp.mean(x * x, axis=-1, keepdims=True) + EPS)


def _dot(a, b):
    return jnp.dot(a, b, preferred_element_type=F32)


def _dot_rows(a, b, parts):
    m = a.shape[0]
    if parts == 1 or m % (parts * MIN_DOT_ROWS):
        return _dot(a, b)
    step = m // parts
    return jnp.concatenate([_dot(a[i * step:(i + 1) * step], b) for i in range(parts)], axis=0)


def _dot_nt(a, b):
    return lax.dot_general(a, b, (((1,), (1,)), ((), ())), preferred_element_type=F32)


def _dot_tn(a, b):
    return lax.dot_general(a, b, (((0,), (0,)), ((), ())), preferred_element_type=F32)


def _neg_abs(x):
    return lax.bitcast_convert_type(
        lax.bitcast_convert_type(x, jnp.uint32) | jnp.uint32(0x80000000), F32)


def _levels(chunk):
    lv, g = [], DIAG
    while 2 * g <= chunk:
        lv.append(g)
        g *= 2
    return lv


def _hgrn_pre(qa, lnk2, va, lf2):
    c = qa.shape[0]
    row = lax.broadcasted_iota(jnp.int32, (c, AW), 0)
    b2 = lf2
    d = 1
    while d < c:
        b2 = b2 + jnp.where(row >= d, pltpu.roll(b2, d, 0), 0.0)
        d *= 2
    c2 = b2 - lnk2
    blast = b2[c - 1:c, :]
    pre = {
        "blast": blast,
        "qc": (qa * jnp.exp2(b2)).astype(BF),
        "kc": jnp.exp2(blast - c2).astype(BF),
        "vb": va.astype(BF),
    }

    nblk = c // SUBLANES
    b3 = b2.reshape(nblk, SUBLANES, AW)
    c3 = c2.reshape(nblk, SUBLANES, AW)
    q3 = qa.reshape(nblk, SUBLANES, AW)
    sub3 = lax.broadcasted_iota(jnp.int32, (nblk, SUBLANES, AW), 1)

    def rows_of_tile(x3, first, stride):
        out = x3[:, first:first + 1, :]
        for i in range(1, SUBLANES // stride):
            out = jnp.where(sub3 >= i * stride, x3[:, first + i * stride:first + i * stride + 1, :], out)
        return jnp.broadcast_to(out, (nblk, SUBLANES, AW))

    xs = []
    for g in _levels(c):
        if 2 * g <= SUBLANES:
            rg = rows_of_tile(b3, g - 1, 2 * g).reshape(c, AW)
        else:
            ref_rows = [jnp.broadcast_to(b2[p * 2 * g + g - 1:p * 2 * g + g, :], (2 * g, AW))
                        for p in range(c // (2 * g))]
            rg = ref_rows[0] if len(ref_rows) == 1 else jnp.concatenate(ref_rows, axis=0)
        odd = (row & g) != 0
        eg = jnp.exp2(_neg_abs(jnp.where(odd, b2, c2) - rg))
        xs.append(jnp.where(odd, qa * eg, eg).astype(BF))
    pre["xs"] = xs
    pre["diag_in"] = (b3, c3, q3)
    return pre


def _hgrn_diag(pre, zero_bits):
    b3, c3, q3 = pre.pop("diag_in")
    nblk = b3.shape[0]
    c = nblk * SUBLANES
    if zero_bits is not None:
        z3 = jnp.concatenate([zero_bits] * (AW // zero_bits.shape[1]), axis=1)[None]
        q3 = lax.bitcast_convert_type(lax.bitcast_convert_type(q3, jnp.uint32) | z3, F32)
    sub3 = lax.broadcasted_iota(jnp.int32, (nblk, SUBLANES, AW), 1)
    lane_s = lax.broadcasted_iota(jnp.int32, (c, c), 1) & (DIAG - 1)
    dtile = [jnp.zeros((c, c), F32) for _ in range(HEADS)]
    for s in range(DIAG):
        cs = c3[:, s:s + 1, :]
        for i in range(1, SUBLANES // DIAG):
            cs = jnp.where(sub3 >= i * DIAG, c3[:, s + i * DIAG:s + i * DIAG + 1, :], cs)
        p = q3 * jnp.exp2(b3 - cs)
        sel = lane_s == s
        for h in range(HEADS):
            a = jnp.sum(p[:, :, h * DK:(h + 1) * DK], axis=-1, keepdims=True).reshape(c, 1)
            dtile[h] = jnp.where(sel, a, dtile[h])
    pre["dtile"] = dtile


def _hgrn_intra(pre):
    kc, vb, xs = pre.pop("kc"), pre.pop("vb"), pre.pop("xs")
    dtile = pre.pop("dtile")
    c = kc.shape[0]
    ti = lax.broadcasted_iota(jnp.int32, (c, c), 0)
    si = lax.broadcasted_iota(jnp.int32, (c, c), 1)
    txs = ti ^ si
    masks = [((ti & g) != 0) & (txs >= g) & (txs < 2 * g) for g in _levels(c)]
    diag_mask = (txs < DIAG) & (si <= ti)
    heads = [slice(h * DK, (h + 1) * DK) for h in range(HEADS)]
    amats = []
    for h, sl in enumerate(heads):
        amat = jnp.where(diag_mask, dtile[h], 0.0)
        for xg, m in zip(xs, masks):
            amat = jnp.where(m, _dot_nt(xg[:, sl], xg[:, sl]), amat)
        amats.append(amat.astype(BF))
    pre["inc"] = [_dot_tn(vb[:, sl], kc[:, sl]) for sl in heads]
    pre["o_intra"] = [_dot(amats[h], vb[:, sl]) for h, sl in enumerate(heads)]


def _hgrn_recur(pre, st):
    qc, blast = pre["qc"], pre["blast"]
    outs, new_st = [], []
    for h in range(HEADS):
        sl = slice(h * DK, (h + 1) * DK)
        outs.append(_dot_nt(qc[:, sl], st[h].astype(BF)) + pre["o_intra"][h])
        new_st.append(st[h] * jnp.exp2(blast[:, sl]) + pre["inc"][h])
    return jnp.concatenate(outs, axis=-1), new_st


def _conv_taps(ubuf_ref, u3, w, cols, row0):
    lt = u3.shape[1]
    ubuf_ref[:, HIST_PAD + row0:HIST_PAD + row0 + lt, cols] = u3
    return (ubuf_ref[:, HIST_LO + row0:HIST_LO + row0 + lt, cols] * w[0:1, :]
            + ubuf_ref[:, HIST_LO + 1 + row0:HIST_LO + 1 + row0 + lt, cols] * w[1:2, :]
            + u3 * w[2:3, :])


def _conv_carry(ubuf_ref, lt, cols):
    hist = ubuf_ref[:, lt + HIST_LO:lt + HIST_PAD, cols]
    ubuf_ref[:, HIST_LO:HIST_PAD, cols] = hist
    return hist


def _mixer_kernel(layer, nb, lt, sub, chunk,
                  x_ref, ada_ref, sin_ref, cbin_ref, lbl_ref, gpre_ref, gpost_ref, anorm_ref,
                  win_ref, wpa_ref, convw_ref, wpb_ref, wbg_ref, wo_ref,
                  xo_ref, sout_ref, cbout_ref, st_ref, ubuf_ref, hb_ref):
    t = pl.program_id(1)
    d = x_ref.shape[-1]
    cw = cbin_ref.shape[-1]
    sl = lt // sub
    r = nb * sl

    @pl.when(t == 0)
    def _():
        for bi in range(nb):
            for h in range(HEADS):
                st_ref[bi, h] = sin_ref[bi, h].T
        ubuf_ref[:, HIST_LO:HIST_PAD, :] = cbin_ref[...]

    ada = ada_ref[...]
    sh1, gt1 = ada[0], ada[2]
    pre_gain = gpre_ref[...] * (1.0 + ada[1])
    post_gain = gt1 * gpost_ref[...]
    if layer > 0:
        lg = lbl_ref[...]
        e = jnp.exp(lg - jnp.max(lg, axis=0, keepdims=True))
        pr = e / jnp.sum(e, axis=0, keepdims=True)
        lb = jnp.sum(pr[1:layer + 1, :], axis=0, keepdims=True)
        a0 = jnp.log(lb) * LOG2E
        c1 = jnp.log1p(-lb) * LOG2E
    state = {bi: [st_ref[bi, h] for h in range(HEADS)] for bi in range(nb)}
    ctx = [dict() for _ in range(sub)]
    rt_zero = lax.bitcast_convert_type(
        jnp.full((SUBLANES, DK), lax.shift_right_arithmetic(t, jnp.int32(31)), jnp.int32), jnp.uint32)

    def norm_in(k):
        c = ctx[k]
        c["x3"] = x_ref[:, k * sl:(k + 1) * sl, :]
        h3 = (c["x3"] * _rms_scale(c["x3"])) * pre_gain + sh1
        hb_ref[k] = h3.reshape(r, d).astype(BF)

    def proj_a(k):
        ctx[k]["proj"] = _dot_rows(hb_ref[k], win_ref[:, 0:4 * AW], EDGE_PARTS if k == 0 else 1)

    def gates_pre(k):
        c = ctx[k]
        proj = c.pop("proj")
        hq, z, vi, hg = proj[:, 0:AW], proj[:, AW:2 * AW], proj[:, 2 * AW:3 * AW], proj[:, 3 * AW:4 * AW]
        qa = _half_silu(hq)
        z2 = z * LOG2E
        l1 = jnp.log(1.0 + jnp.exp2(_neg_abs(z2))) * LOG2E
        logsig2 = jnp.minimum(z2, 0.0) - l1
        lognsig2 = -jnp.maximum(z2, 0.0) - l1
        if layer == 0:
            lf2, lnk2 = logsig2, lognsig2
        else:
            y = c1 + logsig2
            lf2 = jnp.maximum(a0, y) + jnp.log(1.0 + jnp.exp2(_neg_abs(a0 - y))) * LOG2E
            lnk2 = c1 + lognsig2
        c["gs"] = _half_silu(hg) * jnp.concatenate([anorm_ref[...]] * HEADS, axis=-1)
        c["pre"] = []
        for bi in range(nb):
            for ci in range(sl // chunk):
                r0 = bi * sl + ci * chunk
                rs = slice(r0, r0 + chunk)
                c["pre"].append((bi, _hgrn_pre(qa[rs], lnk2[rs], vi[rs], lf2[rs])))

    def conv_branch(k):
        c = ctx[k]
        projb = _dot(hb_ref[k], win_ref[:, 4 * AW:7 * AW])
        bgt, cgt, xv = projb[:, 0:AW], projb[:, AW:2 * AW], projb[:, 2 * AW:3 * AW]
        u = _conv_taps(ubuf_ref, (cgt * xv).reshape(nb, sl, cw), convw_ref[...], slice(None), k * sl)
        c["bu"] = (bgt * u.reshape(r, cw)).astype(BF)

    def diag(k):
        y_b = ctx[k]["bu"]
        zero_bits = pltpu.bitcast(y_b[0:2 * SUBLANES, 0:DK], jnp.uint32) & rt_zero
        for _, pre in ctx[k]["pre"]:
            _hgrn_diag(pre, zero_bits)

    def gate_mm(k):
        c = ctx[k]
        probe = pltpu.bitcast(c["pre"][0][1]["xs"][0][0:2 * SUBLANES, 0:DK], jnp.uint32) & rt_zero
        hb32 = pltpu.bitcast(hb_ref[k], jnp.uint32)
        rows32 = hb32.shape[0]
        hb32 = hb32.reshape(rows32 // SUBLANES, SUBLANES, d) | jnp.concatenate([probe] * (d // DK), axis=1)[None]
        hb = pltpu.bitcast(hb32.reshape(rows32, d), BF)
        c["tg"] = jnp.tanh(_dot(hb, wbg_ref[...]))
        bu32 = pltpu.bitcast(c.pop("bu"), jnp.uint32)
        bu32 = bu32.reshape(rows32 // SUBLANES, SUBLANES, cw) | jnp.concatenate([probe] * (cw // DK), axis=1)[None]
        c["y_b"] = _dot(pltpu.bitcast(bu32.reshape(rows32, cw), BF), wpb_ref[...])

    def intra(k):
        for _, pre in ctx[k]["pre"]:
            _hgrn_intra(pre)

    def recur(k):
        c = ctx[k]
        o_parts = []
        for bi, pre in c.pop("pre"):
            o, state[bi] = _hgrn_recur(pre, state[bi])
            o_parts.append(o)
        c["o"] = o_parts[0] if len(o_parts) == 1 else jnp.concatenate(o_parts, axis=0)

    def tail(k):
        c = ctx[k]
        o = c.pop("o")
        on = jnp.concatenate(
            [o[:, h * DV:(h + 1) * DV] * _rms_scale(o[:, h * DV:(h + 1) * DV]) for h in range(HEADS)],
            axis=-1) * c.pop("gs")
        y_a = _dot(on.astype(BF), wpa_ref[...])
        tg, y_b = c.pop("tg"), c.pop("y_b")
        mix = (y_a + tg[:, :d] * y_a) + (y_b + tg[:, d:] * y_b)
        m = _dot_rows(mix.astype(BF), wo_ref[...], EDGE_PARTS if k == sub - 1 else 1)
        xo_ref[:, k * sl:(k + 1) * sl, :] = c.pop("x3") + (m * _rms_scale(m)).reshape(nb, sl, d) * post_gain

    stages = [norm_in, proj_a, gates_pre, conv_branch, diag, gate_mm, intra, recur, tail]
    for step in range(len(stages) + sub - 1):
        for k in range(sub):
            if 0 <= step - k < len(stages):
                stages[step - k](k)

    cbout_ref[...] = _conv_carry(ubuf_ref, lt, slice(None))
    for bi in range(nb):
        for h in range(HEADS):
            st_ref[bi, h] = state[bi][h]

    @pl.when(t == pl.num_programs(1) - 1)
    def _():
        for bi in range(nb):
            for h in range(HEADS):
                sout_ref[bi, h] = st_ref[bi, h].T


def _ffn_kernel(nb, lt, sub,
                x_ref, ada_ref, fbin_ref, gpre_ref, gpost_ref, wup_ref, convw_ref, wdn_ref,
                xo_ref, fbout_ref, ubuf_ref, hb_ref):
    d = x_ref.shape[-1]
    dff = fbin_ref.shape[-1]
    sl = lt // sub
    r = nb * sl
    cc = FFN_COLS
    nch = dff // cc

    @pl.when(pl.program_id(1) == 0)
    def _():
        ubuf_ref[:, HIST_LO:HIST_PAD, :] = fbin_ref[...] * 0.5

    ada = ada_ref[...]
    sh2 = ada[3]
    pre_gain = gpre_ref[...] * (1.0 + ada[4])
    post_gain = ada[5] * gpost_ref[...]
    x3s, hbs, ffs = {}, {}, {}

    def norm_in(k):
        x3s[k] = x_ref[:, k * sl:(k + 1) * sl, :]
        h3 = (x3s[k] * _rms_scale(x3s[k])) * pre_gain + sh2
        hb_ref[k] = h3.reshape(r, d).astype(BF)
        hbs[k] = hb_ref.at[k]

    def up(k, j):
        c0 = j * cc
        parts = EDGE_PARTS if (k, j) == (0, 0) else 1
        return (_dot_rows(hbs[k][...], wup_ref[:, c0:c0 + cc], parts),
                _dot_rows(hbs[k][...], wup_ref[:, dff + c0:dff + c0 + cc], parts))

    def finish(k):
        ff = ffs.pop(k)
        xo_ref[:, k * sl:(k + 1) * sl, :] = x3s.pop(k) + (ff * _rms_scale(ff)).reshape(nb, sl, d) * post_gain

    items = [(k, j) for k in range(sub) for j in range(nch)]
    norm_in(0)
    nxt = up(0, 0)
    for i, (k, j) in enumerate(items):
        hgate, val = nxt
        if i + 1 < len(items):
            kn, jn = items[i + 1]
            if jn == 0:
                norm_in(kn)
            nxt = up(kn, jn)
        cols = slice(j * cc, (j + 1) * cc)
        hgc = _conv_taps(ubuf_ref, hgate.reshape(nb, sl, cc), convw_ref[:, cols], cols, k * sl)
        act = _half_silu(hgc.reshape(r, cc)) * val
        part = _dot_rows(act.astype(BF), wdn_ref[cols, :], EDGE_PARTS if i + 1 == len(items) else 1)
        ffs[k] = part if j == 0 else ffs[k] + part
        if j == 0 and k > 0:
            finish(k - 1)
    finish(sub - 1)
    fbout_ref[...] = _conv_carry(ubuf_ref, lt, slice(None)) * 2.0


def _ada_kernel(c_ref, w_ref, b_ref, o_ref):
    c = c_ref[...]
    cs = (c / (1.0 + jnp.exp(-c))).astype(BF)
    o_ref[...] = _dot(cs, w_ref[...].astype(BF)) + b_ref[...]


def _ada_call(c_all, w_ada, b_ada):
    depth, d, d6 = w_ada.shape
    n = d6 // d
    bt = c_all.shape[0]
    return pl.pallas_call(
        _ada_kernel,
        grid=(depth, n),
        in_specs=[
            pl.BlockSpec((bt, d), lambda l, j: (0, 0)),
            pl.BlockSpec((None, d, d), lambda l, j: (l, 0, j)),
            pl.BlockSpec((None, None, 1, d), lambda l, j: (l, j, 0, 0)),
        ],
        out_specs=pl.BlockSpec((None, None, bt, d), lambda l, j: (l, j, 0, 0)),
        out_shape=jax.ShapeDtypeStruct((depth, n, bt, d), F32),
        compiler_params=pltpu.CompilerParams(dimension_semantics=("arbitrary", "arbitrary")),
        name="ada",
    )(c_all, w_ada, b_ada.reshape(depth, n, 1, d))


def _resident(shape, layer):
    nd = len(shape)
    return pl.BlockSpec((None,) + tuple(shape[1:]), lambda b, t: (layer,) + (0,) * (nd - 1),
                        pipeline_mode=pl.Buffered(1))


def _tile_plan(batch, length, sub_rows, subs):
    if length % (sub_rows * subs) == 0:
        return 1, sub_rows * subs, subs, HGRN_CHUNK
    assert length % SUBLANES == 0 and length <= sub_rows
    return batch, length, 1, length


def _mixer_call(layer, x, ada, boff, s_in, cb_in, p):
    batch, length, d = x.shape
    nb, lt, sub, chunk = _tile_plan(batch, length, MIXER_SUB_ROWS, MIXER_SUBS)
    assert boff % nb == 0 and batch % nb == 0
    bo = boff // nb
    cw = cb_in.shape[-1]
    row = lambda b, t: (b, t, 0)
    per_b4 = lambda b, t: (b, 0, 0, 0)
    per_b3 = lambda b, t: (b, 0, 0)
    in_specs = [
        pl.BlockSpec((nb, lt, d), row),
        pl.BlockSpec((None, 6, nb, 1, d), lambda b, t: (layer, 0, b + bo, 0, 0)),
        pl.BlockSpec((nb, HEADS, DK, DV), per_b4),
        pl.BlockSpec((nb, CONV_HIST, cw), per_b3),
        pl.BlockSpec(p["lb_logits"].shape, lambda b, t: (0, 0)),
        _resident(p["g_pre_mix"].shape, layer),
        _resident(p["g_post_mix"].shape, layer),
        _resident(p["hgrn_norm"].shape, layer),
        _resident(p["w_in"].shape, layer),
        _resident(p["w_pa"].shape, layer),
        _resident(p["conv_w"].shape, layer),
        _resident(p["w_pb"].shape, layer),
        _resident(p["w_bgate"].shape, layer),
        _resident(p["w_o"].shape, layer),
    ]
    out_specs = [
        pl.BlockSpec((nb, lt, d), row),
        pl.BlockSpec((nb, HEADS, DK, DV), per_b4),
        pl.BlockSpec((nb, CONV_HIST, cw), per_b3),
    ]
    out_shape = [
        jax.ShapeDtypeStruct(x.shape, F32),
        jax.ShapeDtypeStruct(s_in.shape, F32),
        jax.ShapeDtypeStruct(cb_in.shape, F32),
    ]
    return pl.pallas_call(
        functools.partial(_mixer_kernel, layer, nb, lt, sub, chunk),
        grid=(batch // nb, length // lt),
        in_specs=in_specs,
        out_specs=out_specs,
        out_shape=out_shape,
        scratch_shapes=[
            pltpu.VMEM((nb, HEADS, DV, DK), F32),
            pltpu.VMEM((nb, lt + HIST_PAD, cw), F32),
            pltpu.VMEM((sub, nb * lt // sub, d), BF),
        ],
        compiler_params=pltpu.CompilerParams(
            dimension_semantics=("arbitrary", "arbitrary"), vmem_limit_bytes=VMEM_LIMIT),
        name=f"mixer_l{layer}_n{nb}",
    )(x, ada, s_in, cb_in, p["lb_logits"], p["g_pre_mix"], p["g_post_mix"], p["hgrn_norm"],
      p["w_in"], p["w_pa"], p["conv_w"], p["w_pb"], p["w_bgate"], p["w_o"])


def _ffn_call(layer, x, ada, boff, fb_in, p):
    batch, length, d = x.shape
    nb, lt, sub, _ = _tile_plan(batch, length, FFN_SUB_ROWS, FFN_SUBS)
    bo = boff // nb
    dff = fb_in.shape[-1]
    assert dff % FFN_COLS == 0
    row = lambda b, t: (b, t, 0)
    per_b3 = lambda b, t: (b, 0, 0)
    in_specs = [
        pl.BlockSpec((nb, lt, d), row),
        pl.BlockSpec((None, 6, nb, 1, d), lambda b, t: (layer, 0, b + bo, 0, 0)),
        pl.BlockSpec((nb, CONV_HIST, dff), per_b3),
        _resident(p["g_pre_ffn"].shape, layer),
        _resident(p["g_post_ffn"].shape, layer),
        _resident(p["w_up"].shape, layer),
        _resident(p["ffn_conv_w"].shape, layer),
        _resident(p["w_down"].shape, layer),
    ]
    out_specs = [
        pl.BlockSpec((nb, lt, d), row),
        pl.BlockSpec((nb, CONV_HIST, dff), per_b3),
    ]
    out_shape = [
        jax.ShapeDtypeStruct(x.shape, F32),
        jax.ShapeDtypeStruct(fb_in.shape, F32),
    ]
    return pl.pallas_call(
        functools.partial(_ffn_kernel, nb, lt, sub),
        grid=(batch // nb, length // lt),
        in_specs=in_specs,
        out_specs=out_specs,
        out_shape=out_shape,
        scratch_shapes=[pltpu.VMEM((nb, lt + HIST_PAD, dff), F32),
                        pltpu.VMEM((sub, nb * lt // sub, d), BF)],
        compiler_params=pltpu.CompilerParams(
            dimension_semantics=("arbitrary", "arbitrary"), vmem_limit_bytes=VMEM_LIMIT),
        name=f"ffn_l{layer}_n{nb}",
    )(x, ada, fb_in, p["g_pre_ffn"], p["g_post_ffn"], p["w_up"], p["ffn_conv_w"], p["w_down"])


def _trunk(x, ada, boff, s_in, cb_in, fb_in, p):
    depth = p["w_in"].shape[0]
    new_s, new_cb, new_fb = [], [], []
    for l in range(depth):
        x, s, cb = _mixer_call(l, x, ada, boff, s_in[l], cb_in[l], p)
        x, fb = _ffn_call(l, x, ada, boff, fb_in[l], p)
        new_s.append(s)
        new_cb.append(cb)
        new_fb.append(fb)
    return x, jnp.stack(new_s), jnp.stack(new_cb), jnp.stack(new_fb)


def _halve_cols(w, ranges):
    scale = jnp.ones((w.shape[-1],), F32)
    for lo, hi in ranges:
        scale = scale.at[lo:hi].set(0.5)
    return (w * scale).astype(BF)


def kernel(x_prompt, x_sample, state_hgrn, state_conv, state_ffn_conv, c_prompt, c_sample, w_ada, b_ada, g_pre_mix, g_post_mix, g_pre_ffn, g_post_ffn, w_in, hgrn_lb_logits, hgrn_norm, w_pa, conv_w, w_pb, w_bgate, w_o, w_up, ffn_conv_w, w_down):
    depth, d = g_pre_mix.shape
    dff = w_down.shape[1]
    vec = lambda g: g.reshape(depth, 1, g.shape[-1])
    p = {
        "lb_logits": hgrn_lb_logits,
        "g_pre_mix": vec(g_pre_mix), "g_post_mix": vec(g_post_mix),
        "g_pre_ffn": vec(g_pre_ffn), "g_post_ffn": vec(g_post_ffn),
        "hgrn_norm": vec(hgrn_norm),
        "w_in": _halve_cols(w_in, [(0, AW), (3 * AW, 4 * AW)]),
        "w_pa": w_pa.astype(BF), "conv_w": conv_w, "w_pb": w_pb.astype(BF),
        "w_bgate": _halve_cols(w_bgate, [(0, w_bgate.shape[-1])]),
        "w_o": _halve_cols(w_o, [(0, w_o.shape[-1])]),
        "w_up": _halve_cols(w_up, [(0, dff)]), "ffn_conv_w": ffn_conv_w, "w_down": w_down.astype(BF),
    }
    bp = x_prompt.shape[0]
    ada = _ada_call(jnp.concatenate([c_prompt, c_sample], axis=0), w_ada, b_ada)
    ada = ada.reshape(ada.shape[:3] + (1, d))

    s0 = jnp.zeros((depth, bp) + state_hgrn.shape[2:], F32)
    cb0 = jnp.zeros((depth, bp) + state_conv.shape[2:], F32)
    fb0 = jnp.zeros((depth, bp) + state_ffn_conv.shape[2:], F32)
    y_p, s_p, cb_p, fb_p = _trunk(x_prompt, ada, 0, s0, cb0, fb0, p)
    y_s, s_s, cb_s, fb_s = _trunk(x_sample, ada, bp, state_hgrn, state_conv, state_ffn_conv, p)
    return (y_p, y_s, s_p, cb_p, fb_p, s_s, cb_s, fb_s)
```
